```python
import jax, jax.numpy as jnp
from jax import lax
import numpy as np

D_MODEL = 4096
BATCH = 4
SEQ = 4096
DEPTH = 4

GLA_HEADS = 8
GLA_DK = 128
GLA_DV = 256
GLA_QK_WIDTH = GLA_HEADS * GLA_DK
GLA_V_WIDTH = GLA_HEADS * GLA_DV
GLA_DECAY_RANK = 16
GLA_GATE_NORMALIZER = 16.0
GLA_LOG_DECAY_MIN = -1.0
GLA_CHUNK = 64
SWA_Q_HEADS = 32
SWA_KV_HEADS = 8
SWA_HEAD_DIM = 64
SWA_WINDOW = 128
SWA_Q_WIDTH = SWA_Q_HEADS * SWA_HEAD_DIM
SWA_KV_WIDTH = SWA_KV_HEADS * SWA_HEAD_DIM

IN_SPLITS = (GLA_QK_WIDTH, GLA_QK_WIDTH, GLA_V_WIDTH, GLA_V_WIDTH, GLA_DECAY_RANK,
             SWA_Q_WIDTH, SWA_KV_WIDTH, SWA_KV_WIDTH, SWA_Q_WIDTH, D_MODEL, D_MODEL)
IN_WIDTH = sum(IN_SPLITS)
RMS_EPS = 1e-6

kernel_name = "gla_swa_sink_gated_hybrid"


def rms_norm(x, g):
    xf = x.astype(jnp.float32)
    y = xf * lax.rsqrt(jnp.mean(xf * xf, axis=-1, keepdims=True) + RMS_EPS)
    return (y * g.astype(jnp.float32)).astype(x.dtype)


def split_columns(z, sizes):
    out, off = [], 0
    for s in sizes:
        out.append(z[..., off:off + s])
        off += s
    return out


def gla_mix(q, k, v, log_a):
    B_, S_, H, DK = q.shape
    DV = v.shape[-1]
    C = GLA_CHUNK
    N = S_ // C

    def chunk(t):
        return t.astype(jnp.float32).reshape(B_, N, C, H, t.shape[-1]).transpose(0, 3, 1, 2, 4)

    qf = chunk(q) * (DK ** -0.5)
    kf, vf, la = chunk(k), chunk(v), chunk(log_a)
    b = jnp.cumsum(la, axis=3)
    b_ref = b[:, :, :, C // 2:C // 2 + 1]
    b_last = b[:, :, :, C - 1:C]
    A = jnp.einsum('bhnid,bhnjd->bhnij', qf * jnp.exp(b - b_ref), kf * jnp.exp(b_ref - b))
    causal = jnp.tril(jnp.ones((C, C), dtype=bool))
    A = jnp.where(causal, A, 0.0)
    o_intra = jnp.einsum('bhnij,bhnjv->bhniv', A, vf)
    q_inter = qf * jnp.exp(b)
    k_state = kf * jnp.exp(b_last - b)
    decay = jnp.exp(b_last[:, :, :, 0, :])

    def step(state, xs):
        qn, kn, vn, dn = xs
        o = jnp.einsum('bhcd,bhdv->bhcv', qn, state)
        state = dn[..., None] * state + jnp.einsum('bhcd,bhcv->bhdv', kn, vn)
        return state, o

    xs = (jnp.moveaxis(q_inter, 2, 0), jnp.moveaxis(k_state, 2, 0),
          jnp.moveaxis(vf, 2, 0), jnp.moveaxis(decay, 2, 0))
    s0 = jnp.zeros((B_, H, DK, DV), jnp.float32)
    _, o_inter = lax.scan(step, s0, xs)
    o = o_intra + jnp.moveaxis(o_inter, 0, 2)
    return o.transpose(0, 2, 3, 1, 4).reshape(B_, S_, H, DV)


def swa_sink_mix(q, k, v, sinks):
    B_, S_, HQ, HD = q.shape
    G = SWA_KV_HEADS
    R = HQ // G
    W = SWA_WINDOW
    N = S_ // W
    qb = q.astype(jnp.float32).reshape(B_, N, W, G, R, HD)

    def band(t):
        tp = jnp.pad(t.astype(jnp.float32), ((0, 0), (W, 0), (0, 0), (0, 0))).reshape(B_, N + 1, W, G, HD)
        return jnp.concatenate([tp[:, :-1], tp[:, 1:]], axis=2)

    kb, vb = band(k), band(v)
    s = jnp.einsum('bnqgrd,bnkgd->bngrqk', qb, kb) * (HD ** -0.5)
    qi = jnp.arange(W)[:, None]
    kj = jnp.arange(2 * W)[None, :]
    local = (kj > qi) & (kj <= qi + W)
    key_pos = jnp.arange(N)[:, None, None] * W + kj[None] - W
    valid = local[None] & (key_pos >= 0)
    s = jnp.where(valid[None, :, None, None], s, jnp.finfo(jnp.float32).min)
    sink = sinks.astype(jnp.float32).reshape(1, 1, G, R, 1, 1)
    m = jnp.maximum(jnp.max(s, axis=-1, keepdims=True), sink)
    p = jnp.exp(s - m)
    probs = p / (jnp.sum(p, axis=-1, keepdims=True) + jnp.exp(sink - m))
    o = jnp.einsum('bngrqk,bnkgd->bnqgrd', probs, vb)
    return o.reshape(B_, S_, HQ * HD)


def setup_inputs(seed: int = 0) -> dict:
    key = jax.random.key(seed)
    ks = jax.random.split(key, 13)
    f32 = jnp.float32
    x = jax.random.normal(ks[0], (BATCH, SEQ, D_MODEL), f32)
    norm_gains = 1.0 + 0.02 * jax.random.normal(ks[1], (DEPTH, D_MODEL), f32)
    w_in = jax.random.normal(ks[2], (DEPTH, D_MODEL, IN_WIDTH), f32) * D_MODEL ** -0.5
    b_gates = 0.1 * jax.random.normal(ks[3], (DEPTH, 2, D_MODEL), f32)
    w_decay_up = jax.random.normal(ks[4], (DEPTH, GLA_DECAY_RANK, GLA_QK_WIDTH), f32) * GLA_DECAY_RANK ** -0.5
    b_decay = 0.1 * jax.random.normal(ks[5], (DEPTH, GLA_QK_WIDTH), f32)
    gla_norm_gains = 1.0 + 0.02 * jax.random.normal(ks[6], (DEPTH, GLA_DV), f32)
    sinks = 0.5 * jax.random.normal(ks[7], (DEPTH, SWA_Q_HEADS), f32)
    w_gla_out = jax.random.normal(ks[8], (DEPTH, GLA_V_WIDTH, D_MODEL), f32) * GLA_V_WIDTH ** -0.5
    w_swa_out = jax.random.normal(ks[9], (DEPTH, SWA_Q_WIDTH, D_MODEL), f32) * SWA_Q_WIDTH ** -0.5
    w_out = jax.random.normal(ks[10], (DEPTH, D_MODEL, D_MODEL), f32) * D_MODEL ** -0.5
    final_norm_gain = 1.0 + 0.02 * jax.random.normal(ks[11], (D_MODEL,), f32)
    return {"x": x, "norm_gains": norm_gains, "w_in": w_in, "b_gates": b_gates,
            "w_decay_up": w_decay_up, "b_decay": b_decay, "gla_norm_gains": gla_norm_gains,
            "sinks": sinks, "w_gla_out": w_gla_out, "w_swa_out": w_swa_out,
            "w_out": w_out, "final_norm_gain": final_norm_gain}


def reference(x, norm_gains, w_in, b_gates, w_decay_up, b_decay, gla_norm_gains,
              sinks, w_gla_out, w_swa_out, w_out, final_norm_gain):
    B_, S_, _ = x.shape
    for l in range(DEPTH):
        h = rms_norm(x, norm_gains[l])
        z = h @ w_in[l]
        (gq, gk, gv, gg, gdec, sq, sk, sv, sg, ga, gb) = split_columns(z, IN_SPLITS)
        logit = (gdec @ w_decay_up[l] + b_decay[l]).astype(jnp.float32)
        log_a = jnp.maximum(jax.nn.log_sigmoid(logit) / GLA_GATE_NORMALIZER, GLA_LOG_DECAY_MIN)
        o_a = gla_mix(gq.reshape(B_, S_, GLA_HEADS, GLA_DK),
                      gk.reshape(B_, S_, GLA_HEADS, GLA_DK),
                      gv.reshape(B_, S_, GLA_HEADS, GLA_DV),
                      log_a.reshape(B_, S_, GLA_HEADS, GLA_DK))
        o_a = rms_norm(o_a, gla_norm_gains[l]).reshape(B_, S_, GLA_V_WIDTH).astype(x.dtype)
        y_a = (o_a * jax.nn.silu(gg)) @ w_gla_out[l]
        o_b = swa_sink_mix(sq.reshape(B_, S_, SWA_Q_HEADS, SWA_HEAD_DIM),
                           sk.reshape(B_, S_, SWA_KV_HEADS, SWA_HEAD_DIM),
                           sv.reshape(B_, S_, SWA_KV_HEADS, SWA_HEAD_DIM),
                           sinks[l]).astype(x.dtype)
        y_b = (o_b * jax.nn.silu(sg)) @ w_swa_out[l]
        merged = jax.nn.sigmoid(ga + b_gates[l, 0]) * y_a + jax.nn.sigmoid(gb + b_gates[l, 1]) * y_b
        x = x + merged @ w_out[l]
    return rms_norm(x, final_norm_gain)
```

```python
import functools

import jax
import jax.numpy as jnp
from jax import lax
from jax.experimental import pallas as pl
from jax.experimental.pallas import tpu as pltpu

F32 = jnp.float32
BF16 = jnp.bfloat16

D_MODEL = 4096
GLA_HEADS = 8
GLA_DK = 128
GLA_DV = 256
GLA_QK_WIDTH = GLA_HEADS * GLA_DK
GLA_V_WIDTH = GLA_HEADS * GLA_DV
GLA_DECAY_RANK = 16
GLA_GATE_NORMALIZER = 16.0
GLA_LOG_DECAY_MIN = -1.0
GLA_CHUNK = 64
SWA_Q_HEADS = 32
SWA_KV_HEADS = 8
SWA_HEAD_DIM = 64
SWA_WINDOW = 128
SWA_Q_WIDTH = SWA_Q_HEADS * SWA_HEAD_DIM
SWA_KV_WIDTH = SWA_KV_HEADS * SWA_HEAD_DIM
RMS_EPS = 1e-6

LANES = 128
DECAY_PAD = LANES

Z_GQ = 0
Z_GK = Z_GQ + GLA_QK_WIDTH
Z_GV = Z_GK + GLA_QK_WIDTH
Z_GG = Z_GV + GLA_V_WIDTH
Z_SQ = Z_GG + GLA_V_WIDTH
Z_SG = Z_SQ + SWA_Q_WIDTH
Z_SK = Z_SG + SWA_Q_WIDTH
Z_SV = Z_SK + SWA_KV_WIDTH
Z_GA = Z_SV + SWA_KV_WIDTH
Z_GB = Z_GA + D_MODEL
Z_WIDTH = Z_GB + D_MODEL
W_IN_DECAY = 2 * GLA_QK_WIDTH + 2 * GLA_V_WIDTH
W_IN_SQ = W_IN_DECAY + GLA_DECAY_RANK
W_IN_SK = W_IN_SQ + SWA_Q_WIDTH
W_IN_SG = W_IN_SK + 2 * SWA_KV_WIDTH
W_IN_GA = W_IN_SG + SWA_Q_WIDTH

VMEM_LIMIT_BYTES = 56 * 1024 * 1024

NEG_BIG = -1e30


def _params(semantics):
    return pltpu.CompilerParams(dimension_semantics=semantics,
                                vmem_limit_bytes=VMEM_LIMIT_BYTES)


def _rms_rows(x, gain):
    ms = jnp.mean(x * x, axis=-1, keepdims=True)
    return x * lax.rsqrt(ms + RMS_EPS) * gain


def _log_sigmoid(x):
    return jnp.minimum(x, 0.0) - jnp.log(1.0 + jnp.exp(-jnp.abs(x)))


def _norm_decay_kernel(x_ref, g_ref, wdec_ref, wup_ref, bdec_ref, h_ref, la_ref):
    h = _rms_rows(x_ref[...], g_ref[...]).astype(BF16)
    h_ref[...] = h
    gdec = jnp.dot(h, wdec_ref[...], preferred_element_type=F32)
    logit = jnp.dot(gdec.astype(BF16), wup_ref[...], preferred_element_type=F32) + bdec_ref[...]
    la_ref[...] = jnp.maximum(_log_sigmoid(logit) / GLA_GATE_NORMALIZER, GLA_LOG_DECAY_MIN)


def norm_decay(x, gain, wdec, wup, bdec, *, tm=512):
    t, d = x.shape
    tm = min(tm, t)
    return pl.pallas_call(
        _norm_decay_kernel,
        grid=(t // tm,),
        in_specs=[
            pl.BlockSpec((tm, d), lambda i: (i, 0)),
            pl.BlockSpec((1, d), lambda i: (0, 0)),
            pl.BlockSpec((d, DECAY_PAD), lambda i: (0, 0)),
            pl.BlockSpec((DECAY_PAD, GLA_QK_WIDTH), lambda i: (0, 0)),
            pl.BlockSpec((1, GLA_QK_WIDTH), lambda i: (0, 0)),
        ],
        out_specs=[
            pl.BlockSpec((tm, d), lambda i: (i, 0)),
            pl.BlockSpec((tm, GLA_QK_WIDTH), lambda i: (i, 0)),
        ],
        out_shape=[
            jax.ShapeDtypeStruct((t, d), BF16),
            jax.ShapeDtypeStruct((t, GLA_QK_WIDTH), F32),
        ],
        compiler_params=_params(("arbitrary",)),
        name="norm_decay",
    )(x, gain, wdec, wup, bdec)


def _final_norm_kernel(x_ref, g_ref, o_ref):
    o_ref[...] = _rms_rows(x_ref[...], g_ref[...])


def final_norm(x, gain, *, tm=512):
    t, d = x.shape
    tm = min(tm, t)
    return pl.pallas_call(
        _final_norm_kernel,
        grid=(t // tm,),
        in_specs=[pl.BlockSpec((tm, d), lambda i: (i, 0)),
                  pl.BlockSpec((1, d), lambda i: (0, 0))],
        out_specs=pl.BlockSpec((tm, d), lambda i: (i, 0)),
        out_shape=jax.ShapeDtypeStruct((t, d), F32),
        compiler_params=_params(("arbitrary",)),
        name="final_norm",
    )(x, gain)


def _in_proj_kernel(h_ref, w_ref, z_ref):
    z_ref[...] = jnp.dot(h_ref[...], w_ref[...], preferred_element_type=F32).astype(z_ref.dtype)


def in_proj(h, w, *, tm=1024, tn=1024):
    t, d = h.shape
    tm = min(tm, t)
    n = w.shape[1]
    return pl.pallas_call(
        _in_proj_kernel,
        grid=(t // tm, n // tn),
        in_specs=[pl.BlockSpec((tm, d), lambda i, j: (i, 0)),
                  pl.BlockSpec((d, tn), lambda i, j: (0, j))],
        out_specs=pl.BlockSpec((tm, tn), lambda i, j: (i, j)),
        out_shape=jax.ShapeDtypeStruct((t, n), BF16),
        compiler_params=_params(("arbitrary", "arbitrary")),
        name="in_proj",
    )(h, w)


def _out_proj_kernel(m_ref, w_ref, x_ref, o_ref):
    o_ref[...] = x_ref[...] + jnp.dot(m_ref[...], w_ref[...], preferred_element_type=F32)


def out_proj(merged, w, x, *, tm=1024, tn=1024):
    t, d = merged.shape
    tm = min(tm, t)
    n = w.shape[1]
    return pl.pallas_call(
        _out_proj_kernel,
        grid=(t // tm, n // tn),
        in_specs=[pl.BlockSpec((tm, d), lambda i, j: (i, 0)),
                  pl.BlockSpec((d, tn), lambda i, j: (0, j)),
                  pl.BlockSpec((tm, tn), lambda i, j: (i, j))],
        out_specs=pl.BlockSpec((tm, tn), lambda i, j: (i, j)),
        out_shape=jax.ShapeDtypeStruct((t, n), F32),
        compiler_params=_params(("arbitrary", "arbitrary")),
        name="out_proj",
    )(merged, w, x)


def _merge_kernel(ua_ref, ub_ref, wa_ref, wb_ref, ga_ref, gb_ref, bg_ref, o_ref):
    ya = jnp.dot(ua_ref[...], wa_ref[...], preferred_element_type=F32)
    yb = jnp.dot(ub_ref[...], wb_ref[...], preferred_element_type=F32)
    bg = bg_ref[...]
    sa = jax.nn.sigmoid(ga_ref[...].astype(F32) + bg[0:1, :])
    sb = jax.nn.sigmoid(gb_ref[...].astype(F32) + bg[1:2, :])
    o_ref[...] = (sa * ya + sb * yb).astype(o_ref.dtype)


def merge(ua, ub, wa, wb, z, bgates, *, tm=1024, tn=1024):
    t, ka = ua.shape
    tm = min(tm, t)
    kb = ub.shape[1]
    n = wa.shape[1]
    ga_blk = Z_GA // tn
    gb_blk = Z_GB // tn
    return pl.pallas_call(
        _merge_kernel,
        grid=(t // tm, n // tn),
        in_specs=[pl.BlockSpec((tm, ka), lambda i, j: (i, 0)),
                  pl.BlockSpec((tm, kb), lambda i, j: (i, 0)),
                  pl.BlockSpec((ka, tn), lambda i, j: (0, j)),
                  pl.BlockSpec((kb, tn), lambda i, j: (0, j)),
                  pl.BlockSpec((tm, tn), lambda i, j: (i, ga_blk + j)),
                  pl.BlockSpec((tm, tn), lambda i, j: (i, gb_blk + j)),
                  pl.BlockSpec((2, tn), lambda i, j: (0, j))],
        out_specs=pl.BlockSpec((tm, tn), lambda i, j: (i, j)),
        out_shape=jax.ShapeDtypeStruct((t, n), BF16),
        compiler_params=_params(("arbitrary", "arbitrary")),
        name="merge",
    )(ua, ub, wa, wb, z, z, bgates)


def _cumsum_rows(x):
    n = x.shape[0]
    row = lax.broadcasted_iota(jnp.int32, x.shape, 0)
    s = 1
    while s < n:
        x = x + jnp.where(row >= s, pltpu.roll(x, s, axis=0), 0.0)
        s *= 2
    return x


def _gla_kernel(q_ref, k_ref, v_ref, gg_ref, la_ref, gain_ref, o_ref, st_ref, *, rows):
    @pl.when(pl.program_id(1) == 0)
    def _():
        st_ref[...] = jnp.zeros_like(st_ref)

    c_sz = GLA_CHUNK
    scale = GLA_DK ** -0.5
    ri = lax.broadcasted_iota(jnp.int32, (c_sz, c_sz), 0)
    ci = lax.broadcasted_iota(jnp.int32, (c_sz, c_sz), 1)
    causal = ci <= ri
    gain = gain_ref[...]
    nt = (((1,), (1,)), ((), ()))
    tn = (((0,), (0,)), ((), ()))

    for h in range(GLA_HEADS):
        kcols = slice(h * GLA_DK, (h + 1) * GLA_DK)
        vcols = slice(h * GLA_DV, (h + 1) * GLA_DV)
        for c in range(rows // c_sz):
            r = slice(c * c_sz, (c + 1) * c_sz)
            qf = q_ref[r, kcols].astype(F32) * scale
            kf = k_ref[r, kcols].astype(F32)
            v = v_ref[r, vcols]
            b = _cumsum_rows(la_ref[r, kcols])
            b_ref = b[c_sz // 2:c_sz // 2 + 1, :]
            b_last = b[c_sz - 1:c_sz, :]
            q_intra = (qf * jnp.exp(b - b_ref)).astype(BF16)
            k_intra = (kf * jnp.exp(b_ref - b)).astype(BF16)
            q_inter = (qf * jnp.exp(b)).astype(BF16)
            k_state = (kf * jnp.exp(b_last - b)).astype(BF16)
            decay = jnp.exp(b_last)

            a = lax.dot_general(q_intra, k_intra, nt, preferred_element_type=F32)
            a = jnp.where(causal, a, 0.0).astype(BF16)
            st = st_ref[h]
            o = jnp.dot(a, v, preferred_element_type=F32)
            o = o + lax.dot_general(q_inter, st.astype(BF16), nt, preferred_element_type=F32)
            st_ref[h] = st * decay + lax.dot_general(v, k_state, tn, preferred_element_type=F32)

            y = _rms_rows(o, gain)
            g = gg_ref[r, vcols].astype(F32)
            o_ref[r, vcols] = (y * (g * jax.nn.sigmoid(g))).astype(o_ref.dtype)


def gla(z, la, gain, *, batch, rows=128):
    t = z.shape[0]
    steps = t // batch // rows
    qw, vw = GLA_QK_WIDTH, GLA_V_WIDTH
    row_map = lambda b, i: b * steps + i
    return pl.pallas_call(
        functools.partial(_gla_kernel, rows=rows),
        grid=(batch, steps),
        in_specs=[
            pl.BlockSpec((rows, qw), lambda b, i: (row_map(b, i), Z_GQ // qw)),
            pl.BlockSpec((rows, qw), lambda b, i: (row_map(b, i), Z_GK // qw)),
            pl.BlockSpec((rows, vw), lambda b, i: (row_map(b, i), Z_GV // vw)),
            pl.BlockSpec((rows, vw), lambda b, i: (row_map(b, i), Z_GG // vw)),
            pl.BlockSpec((rows, qw), lambda b, i: (row_map(b, i), 0)),
            pl.BlockSpec((1, GLA_DV), lambda b, i: (0, 0)),
        ],
        out_specs=pl.BlockSpec((rows, vw), lambda b, i: (row_map(b, i), 0)),
        out_shape=jax.ShapeDtypeStruct((t, vw), BF16),
        scratch_shapes=[pltpu.VMEM((GLA_HEADS, GLA_DV, GLA_DK), F32)],
        compiler_params=_params(("arbitrary", "arbitrary")),
        name="gla",
    )(z, z, z, z, la, gain)


def _swa_kernel(sink_ref, q_ref, kp_ref, kc_ref, vp_ref, vc_ref, sg_ref, o_ref):
    w = SWA_WINDOW
    hd = SWA_HEAD_DIM
    rep = SWA_Q_HEADS // SWA_KV_HEADS
    scale = hd ** -0.5
    first_key = jnp.where(pl.program_id(1) > 0, 0, w)

    qi = lax.broadcasted_iota(jnp.int32, (w, 2 * w), 0)
    kj = lax.broadcasted_iota(jnp.int32, (w, 2 * w), 1)
    valid = (kj > qi) & (kj <= qi + w) & (kj >= first_key)
    lane = lax.broadcasted_iota(jnp.int32, (2 * w, LANES), 1)
    lo = lane < hd
    nt = (((1,), (1,)), ((), ()))

    for pair in range(SWA_KV_HEADS // 2):
        cols = slice(pair * LANES, (pair + 1) * LANES)
        k2 = jnp.concatenate([kp_ref[:, cols], kc_ref[:, cols]], axis=0)
        v2 = jnp.concatenate([vp_ref[:, cols], vc_ref[:, cols]], axis=0)
        k2r = pltpu.roll(k2, hd, axis=1)
        v2r = pltpu.roll(v2, hd, axis=1)
        zero = jnp.zeros_like(k2)
        for sub in range(2):
            g = 2 * pair + sub
            if sub == 0:
                k_lo, k_hi = jnp.where(lo, k2, zero), jnp.where(lo, zero, k2r)
                v_lo, v_hi = jnp.where(lo, v2, zero), jnp.where(lo, zero, v2r)
            else:
                k_lo, k_hi = jnp.where(lo, k2r, zero), jnp.where(lo, zero, k2)
                v_lo, v_hi = jnp.where(lo, v2r, zero), jnp.where(lo, zero, v2)
            for hp in range(rep // 2):
                qcols = slice((g * rep + 2 * hp) * hd, (g * rep + 2 * hp + 2) * hd)
                q2 = q_ref[:, qcols]
                o = None
                for e, (kk, vv) in enumerate(((k_lo, v_lo), (k_hi, v_hi))):
                    sink = sink_ref[g * rep + 2 * hp + e]
                    s = lax.dot_general(q2, kk, nt, preferred_element_type=F32) * scale
                    s = jnp.where(valid, s, NEG_BIG)
                    m = jnp.maximum(jnp.max(s, axis=-1, keepdims=True), sink)
                    p = jnp.exp(s - m)
                    denom = jnp.sum(p, axis=-1, keepdims=True) + jnp.exp(sink - m)
                    part = jnp.dot(p.astype(BF16), vv, preferred_element_type=F32) / denom
                    o = part if o is None else o + part
                gate = sg_ref[:, qcols].astype(F32)
                o_ref[:, qcols] = (o * (gate * jax.nn.sigmoid(gate))).astype(o_ref.dtype)


def swa(z, sinks, *, batch):
    t = z.shape[0]
    w = SWA_WINDOW
    steps = t // batch // w
    qw, kw = SWA_Q_WIDTH, SWA_KV_WIDTH
    cur = lambda b, i: b * steps + i
    prev = lambda b, i: b * steps + jnp.maximum(i - 1, 0)
    return pl.pallas_call(
        _swa_kernel,
        grid=(batch, steps),
        in_specs=[
            pl.BlockSpec(memory_space=pltpu.SMEM),
            pl.BlockSpec((w, qw), lambda b, i: (cur(b, i), Z_SQ // qw)),
            pl.BlockSpec((w, kw), lambda b, i: (prev(b, i), Z_SK // kw)),
            pl.BlockSpec((w, kw), lambda b, i: (cur(b, i), Z_SK // kw)),
            pl.BlockSpec((w, kw), lambda b, i: (prev(b, i), Z_SV // kw)),
            pl.BlockSpec((w, kw), lambda b, i: (cur(b, i), Z_SV // kw)),
            pl.BlockSpec((w, qw), lambda b, i: (cur(b, i), Z_SG // qw)),
        ],
        out_specs=pl.BlockSpec((w, qw), lambda b, i: (cur(b, i), 0)),
        out_shape=jax.ShapeDtypeStruct((t, qw), BF16),
        compiler_params=_params(("arbitrary", "arbitrary")),
        name="swa",
    )(sinks, z, z, z, z, z, z)


def _pack_w_in(w):
    main = jnp.concatenate([w[:, :W_IN_DECAY], w[:, W_IN_SQ:W_IN_SK], w[:, W_IN_SG:W_IN_GA],
                            w[:, W_IN_SK:W_IN_SG], w[:, W_IN_GA:]], axis=1).astype(BF16)
    dec = w[:, W_IN_DECAY:W_IN_SQ]
    dec = jnp.pad(dec, ((0, 0), (0, DECAY_PAD - GLA_DECAY_RANK))).astype(BF16)
    return main, dec


def kernel(x, norm_gains, w_in, b_gates, w_decay_up, b_decay, gla_norm_gains, sinks,
           w_gla_out, w_swa_out, w_out, final_norm_gain):
    batch, seq, d = x.shape
    depth = w_in.shape[0]
    xt = x.reshape(batch * seq, d)
    for l in range(depth):
        w_main, w_dec = _pack_w_in(w_in[l])
        w_up = jnp.pad(w_decay_up[l], ((0, DECAY_PAD - GLA_DECAY_RANK), (0, 0))).astype(BF16)
        h, la = norm_decay(xt, norm_gains[l][None, :], w_dec, w_up, b_decay[l][None, :])
        z = in_proj(h, w_main)
        ua = gla(z, la, gla_norm_gains[l][None, :], batch=batch)
        ub = swa(z, sinks[l], batch=batch)
        merged = merge(ua, ub, w_gla_out[l].astype(BF16), w_swa_out[l].astype(BF16), z, b_gates[l])
        xt = out_proj(merged, w_out[l].astype(BF16), xt)
    out = final_norm(xt, final_norm_gain[None, :])
    return out.reshape(batch, seq, d)
```

```python
import functools

import jax
import jax.numpy as jnp
from jax import lax
from jax.experimental import pallas as pl
from jax.experimental.pallas import tpu as pltpu

F32 = jnp.float32
BF16 = jnp.bfloat16

D_MODEL = 4096
GLA_HEADS = 8
GLA_DK = 128
GLA_DV = 256
GLA_QK_WIDTH = GLA_HEADS * GLA_DK
GLA_V_WIDTH = GLA_HEADS * GLA_DV
GLA_DECAY_RANK = 16
GLA_GATE_NORMALIZER = 16.0
GLA_LOG_DECAY_MIN = -1.0
GLA_CHUNK = 64
SWA_Q_HEADS = 32
SWA_KV_HEADS = 8
SWA_HEAD_DIM = 64
SWA_WINDOW = 128
SWA_Q_WIDTH = SWA_Q_HEADS * SWA_HEAD_DIM
SWA_KV_WIDTH = SWA_KV_HEADS * SWA_HEAD_DIM
RMS_EPS = 1e-6

LANES = 128
DECAY_PAD = LANES

W_IN_DECAY = 2 * GLA_QK_WIDTH + 2 * GLA_V_WIDTH
W_IN_SQ = W_IN_DECAY + GLA_DECAY_RANK
W_IN_SK = W_IN_SQ + SWA_Q_WIDTH
W_IN_SG = W_IN_SK + 2 * SWA_KV_WIDTH
W_IN_GA = W_IN_SG + SWA_Q_WIDTH
Z_GQ = 0
Z_GK = Z_GQ + GLA_QK_WIDTH
Z_GV = Z_GK + GLA_QK_WIDTH
Z_GG = Z_GV + GLA_V_WIDTH
Z_SQ = Z_GG + GLA_V_WIDTH
Z_SG = Z_SQ + SWA_Q_WIDTH
Z_SK = Z_SG + SWA_Q_WIDTH
Z_SV = Z_SK + SWA_KV_WIDTH
Z_GA = Z_SV + SWA_KV_WIDTH
Z_GB = Z_GA + D_MODEL
Z_WIDTH = Z_GB + D_MODEL

VMEM_LIMIT_BYTES = 56 * 1024 * 1024

NEG_BIG = -1e30
LOG2E = 1.4426950408889634


def _params(semantics):
    return pltpu.CompilerParams(dimension_semantics=semantics,
                                vmem_limit_bytes=VMEM_LIMIT_BYTES)


PACK_TN = 512


def _pack_kernel(a_ref, b_ref, o_ref, *, n_direct):
    j = pl.program_id(1)

    @pl.when(j < n_direct)
    def _():
        o_ref[...] = a_ref[...].astype(BF16)

    @pl.when(j >= n_direct)
    def _():
        x = jnp.concatenate([a_ref[...], b_ref[...]], axis=1)
        y = pltpu.roll(x, x.shape[1] - GLA_DECAY_RANK, axis=1)
        o_ref[...] = y[:, :o_ref.shape[1]].astype(BF16)


def pack_w_in(w, *, tr=1024):
    rows = w.shape[0]
    tn = PACK_TN
    n_direct = W_IN_DECAY // tn
    sg_lo, sk_lo, ga_lo = Z_SG // tn, Z_SK // tn, Z_GA // tn
    sg_shift = (W_IN_SG - GLA_DECAY_RANK) // tn - sg_lo
    sk_shift = (W_IN_SK - GLA_DECAY_RANK) // tn - sk_lo

    def src(j):
        return jnp.where((j >= sg_lo) & (j < sk_lo), j + sg_shift,
                         jnp.where((j >= sk_lo) & (j < ga_lo), j + sk_shift, j))

    return pl.pallas_call(
        functools.partial(_pack_kernel, n_direct=n_direct),
        grid=(rows // tr, Z_WIDTH // tn),
        in_specs=[pl.BlockSpec((tr, tn), lambda i, j: (i, src(j))),
                  pl.BlockSpec((tr, LANES), lambda i, j: (i, (src(j) + 1) * (tn // LANES)))],
        out_specs=pl.BlockSpec((tr, tn), lambda i, j: (i, j)),
        out_shape=jax.ShapeDtypeStruct((rows, Z_WIDTH), BF16),
        compiler_params=_params(("arbitrary", "arbitrary")),
        name="pack_w_in",
    )(w, w)


def _cast_kernel(a_ref, o_ref):
    o_ref[...] = a_ref[...].astype(o_ref.dtype)


def cast_bf16(w, *, tr=512):
    rows, cols = w.shape
    return pl.pallas_call(
        _cast_kernel,
        grid=(rows // tr,),
        in_specs=[pl.BlockSpec((tr, cols), lambda i: (i, 0))],
        out_specs=pl.BlockSpec((tr, cols), lambda i: (i, 0)),
        out_shape=jax.ShapeDtypeStruct((rows, cols), BF16),
        compiler_params=_params(("arbitrary",)),
        name="cast_bf16",
    )(w)


def _rms_rows(x, gain):
    ms = jnp.mean(x * x, axis=-1, keepdims=True)
    return x * lax.rsqrt(ms + RMS_EPS) * gain


def _log_sigmoid(x):
    return jnp.minimum(x, 0.0) - jnp.log(1.0 + jnp.exp(-jnp.abs(x)))


def _norm_decay_kernel(x_ref, g_ref, wdec_ref, wup_ref, bdec_ref, h_ref, la_ref):
    h = _rms_rows(x_ref[...], g_ref[...]).astype(BF16)
    h_ref[...] = h
    gdec = jnp.dot(h, wdec_ref[...], preferred_element_type=F32)
    logit = jnp.dot(gdec.astype(BF16), wup_ref[...], preferred_element_type=F32) + bdec_ref[...]
    la_ref[...] = jnp.maximum(_log_sigmoid(logit) / GLA_GATE_NORMALIZER, GLA_LOG_DECAY_MIN)


def norm_decay(x, gain, wdec, wup, bdec, *, tm=512):
    t, d = x.shape
    tm = min(tm, t)
    return pl.pallas_call(
        _norm_decay_kernel,
        grid=(t // tm,),
        in_specs=[
            pl.BlockSpec((tm, d), lambda i: (i, 0)),
            pl.BlockSpec((1, d), lambda i: (0, 0)),
            pl.BlockSpec((d, DECAY_PAD), lambda i: (0, 0)),
            pl.BlockSpec((DECAY_PAD, GLA_QK_WIDTH), lambda i: (0, 0)),
            pl.BlockSpec((1, GLA_QK_WIDTH), lambda i: (0, 0)),
        ],
        out_specs=[
            pl.BlockSpec((tm, d), lambda i: (i, 0)),
            pl.BlockSpec((tm, GLA_QK_WIDTH), lambda i: (i, 0)),
        ],
        out_shape=[
            jax.ShapeDtypeStruct((t, d), BF16),
            jax.ShapeDtypeStruct((t, GLA_QK_WIDTH), F32),
        ],
        compiler_params=_params(("arbitrary",)),
        name="norm_decay",
    )(x, gain, wdec, wup, bdec)


def _final_norm_kernel(x_ref, g_ref, o_ref):
    o_ref[...] = _rms_rows(x_ref[...], g_ref[...])


def final_norm(x, gain, *, tm=512):
    t, d = x.shape
    tm = min(tm, t)
    return pl.pallas_call(
        _final_norm_kernel,
        grid=(t // tm,),
        in_specs=[pl.BlockSpec((tm, d), lambda i: (i, 0)),
                  pl.BlockSpec((1, d), lambda i: (0, 0))],
        out_specs=pl.BlockSpec((tm, d), lambda i: (i, 0)),
        out_shape=jax.ShapeDtypeStruct((t, d), F32),
        compiler_params=_params(("arbitrary",)),
        name="final_norm",
    )(x, gain)


def _in_proj_kernel(h_ref, w_ref, z_ref):
    z_ref[...] = jnp.dot(h_ref[...], w_ref[...], preferred_element_type=F32).astype(z_ref.dtype)


def in_proj(h, w, layer, *, tm=1024, tn=1024):
    t, d = h.shape
    tm = min(tm, t)
    n = w.shape[1]
    return pl.pallas_call(
        _in_proj_kernel,
        grid=(t // tm, n // tn),
        in_specs=[pl.BlockSpec((tm, d), lambda i, j: (i, 0)),
                  pl.BlockSpec((d, tn), lambda i, j: (layer, j))],
        out_specs=pl.BlockSpec((tm, tn), lambda i, j: (i, j)),
        out_shape=jax.ShapeDtypeStruct((t, n), BF16),
        compiler_params=_params(("arbitrary", "arbitrary")),
        name="in_proj",
    )(h, w)


def _out_proj_kernel(m_ref, w_ref, x_ref, o_ref):
    o_ref[...] = x_ref[...] + jnp.dot(m_ref[...], w_ref[...], preferred_element_type=F32)


def out_proj(merged, w, x, layer, *, tm=1024, tn=1024):
    t, d = merged.shape
    tm = min(tm, t)
    n = w.shape[1]
    return pl.pallas_call(
        _out_proj_kernel,
        grid=(t // tm, n // tn),
        in_specs=[pl.BlockSpec((tm, d), lambda i, j: (i, 0)),
                  pl.BlockSpec((d, tn), lambda i, j: (layer, j)),
                  pl.BlockSpec((tm, tn), lambda i, j: (i, j))],
        out_specs=pl.BlockSpec((tm, tn), lambda i, j: (i, j)),
        out_shape=jax.ShapeDtypeStruct((t, n), F32),
        compiler_params=_params(("arbitrary", "arbitrary")),
        name="out_proj",
    )(merged, w, x)


def _merge_kernel(ua_ref, ub_ref, wa_ref, wb_ref, ga_ref, gb_ref, bg_ref, o_ref):
    ya = jnp.dot(ua_ref[...], wa_ref[...], preferred_element_type=F32)
    yb = jnp.dot(ub_ref[...], wb_ref[...], preferred_element_type=F32)
    bg = bg_ref[...]
    sa = jax.nn.sigmoid(ga_ref[...].astype(F32) + bg[0:1, :])
    sb = jax.nn.sigmoid(gb_ref[...].astype(F32) + bg[1:2, :])
    o_ref[...] = (sa * ya + sb * yb).astype(o_ref.dtype)


def merge(ua, ub, wa, wb, z, bgates, layer, *, tm=1024, tn=1024):
    t, ka = ua.shape
    tm = min(tm, t)
    kb = ub.shape[1]
    n = wa.shape[1]
    ga_blk = Z_GA // tn
    gb_blk = Z_GB // tn
    return pl.pallas_call(
        _merge_kernel,
        grid=(t // tm, n // tn),
        in_specs=[pl.BlockSpec((tm, ka), lambda i, j: (i, 0)),
                  pl.BlockSpec((tm, kb), lambda i, j: (i, 0)),
                  pl.BlockSpec((ka, tn), lambda i, j: (layer, j)),
                  pl.BlockSpec((kb, tn), lambda i, j: (layer, j)),
                  pl.BlockSpec((tm, tn), lambda i, j: (i, ga_blk + j)),
                  pl.BlockSpec((tm, tn), lambda i, j: (i, gb_blk + j)),
                  pl.BlockSpec((2, tn), lambda i, j: (0, j))],
        out_specs=pl.BlockSpec((tm, tn), lambda i, j: (i, j)),
        out_shape=jax.ShapeDtypeStruct((t, n), BF16),
        compiler_params=_params(("arbitrary", "arbitrary")),
        name="merge",
    )(ua, ub, wa, wb, z, z, bgates)


def _cumsum_rows(x):
    n = x.shape[0]
    row = lax.broadcasted_iota(jnp.int32, x.shape, 0)
    s = 1
    while s < n:
        x = x + jnp.where(row >= s, pltpu.roll(x, s, axis=0), 0.0)
        s *= 2
    return x


def _gla_kernel(q_ref, k_ref, v_ref, gg_ref, la_ref, gain_ref, o_ref, st_ref, *, rows):
    @pl.when(pl.program_id(1) == 0)
    def _():
        st_ref[...] = jnp.zeros_like(st_ref)

    c_sz = GLA_CHUNK
    scale = GLA_DK ** -0.5
    ri = lax.broadcasted_iota(jnp.int32, (c_sz, c_sz), 0)
    ci = lax.broadcasted_iota(jnp.int32, (c_sz, c_sz), 1)
    causal = ci <= ri
    gain = gain_ref[...]
    nt = (((1,), (1,)), ((), ()))
    tn = (((0,), (0,)), ((), ()))

    for h in range(GLA_HEADS):
        kcols = slice(h * GLA_DK, (h + 1) * GLA_DK)
        vcols = slice(h * GLA_DV, (h + 1) * GLA_DV)
        for c in range(rows // c_sz):
            r = slice(c * c_sz, (c + 1) * c_sz)
            qf = q_ref[r, kcols].astype(F32) * scale
            kf = k_ref[r, kcols].astype(F32)
            v = v_ref[r, vcols]
            b = _cumsum_rows(la_ref[r, kcols]) * LOG2E
            b_ref = b[c_sz // 2:c_sz // 2 + 1, :]
            b_last = b[c_sz - 1:c_sz, :]
            q_intra = (qf * jnp.exp2(b - b_ref)).astype(BF16)
            k_intra = (kf * jnp.exp2(b_ref - b)).astype(BF16)
            q_inter = (qf * jnp.exp2(b)).astype(BF16)
            k_state = (kf * jnp.exp2(b_last - b)).astype(BF16)
            decay = jnp.exp2(b_last)

            a = lax.dot_general(q_intra, k_intra, nt, preferred_element_type=F32)
            a = jnp.where(causal, a, 0.0).astype(BF16)
            st = st_ref[h]
            o = jnp.dot(a, v, preferred_element_type=F32)
            o = o + lax.dot_general(q_inter, st.astype(BF16), nt, preferred_element_type=F32)
            st_ref[h] = st * decay + lax.dot_general(v, k_state, tn, preferred_element_type=F32)

            y = _rms_rows(o, gain)
            g = gg_ref[r, vcols].astype(F32)
            o_ref[r, vcols] = (y * (g * jax.nn.sigmoid(g))).astype(o_ref.dtype)


def gla(z, la, gain, *, batch, rows=256):
    t = z.shape[0]
    steps = t // batch // rows
    qw, vw = GLA_QK_WIDTH, GLA_V_WIDTH
    row_map = lambda b, i: b * steps + i
    return pl.pallas_call(
        functools.partial(_gla_kernel, rows=rows),
        grid=(batch, steps),
        in_specs=[
            pl.BlockSpec((rows, qw), lambda b, i: (row_map(b, i), Z_GQ // qw)),
            pl.BlockSpec((rows, qw), lambda b, i: (row_map(b, i), Z_GK // qw)),
            pl.BlockSpec((rows, vw), lambda b, i: (row_map(b, i), Z_GV // vw)),
            pl.BlockSpec((rows, vw), lambda b, i: (row_map(b, i), Z_GG // vw)),
            pl.BlockSpec((rows, qw), lambda b, i: (row_map(b, i), 0)),
            pl.BlockSpec((1, GLA_DV), lambda b, i: (0, 0)),
        ],
        out_specs=pl.BlockSpec((rows, vw), lambda b, i: (row_map(b, i), 0)),
        out_shape=jax.ShapeDtypeStruct((t, vw), BF16),
        scratch_shapes=[pltpu.VMEM((GLA_HEADS, GLA_DV, GLA_DK), F32)],
        compiler_params=_params(("arbitrary", "arbitrary")),
        name="gla",
    )(z, z, z, z, la, gain)


def _swa_kernel(sink_ref, q_ref, kp_ref, kc_ref, vp_ref, vc_ref, sg_ref, o_ref):
    w = SWA_WINDOW
    hd = SWA_HEAD_DIM
    rep = SWA_Q_HEADS // SWA_KV_HEADS
    scale = hd ** -0.5
    first_prev_key = jnp.where(pl.program_id(1) > 0, 0, w)

    qi = lax.broadcasted_iota(jnp.int32, (w, w), 0)
    kj = lax.broadcasted_iota(jnp.int32, (w, w), 1)
    upper = kj > qi
    use_prev = upper & (kj >= first_prev_key)
    lane = lax.broadcasted_iota(jnp.int32, (w, LANES), 1)
    head_lanes = (lane < hd, lane >= hd)
    nt = (((1,), (1,)), ((), ()))

    for pair in range(SWA_KV_HEADS // 2):
        cols = slice(pair * LANES, (pair + 1) * LANES)
        k2 = jnp.concatenate([kp_ref[:, cols], kc_ref[:, cols]], axis=0) * scale
        v2 = jnp.concatenate([vp_ref[:, cols], vc_ref[:, cols]], axis=0)
        k2r = pltpu.roll(k2, hd, axis=1)
        v2r = pltpu.roll(v2, hd, axis=1)
        for sub in range(2):
            g = 2 * pair + sub
            for hp in range(rep // 2):
                head0 = g * rep + 2 * hp
                qcols = slice(head0 * hd, (head0 + 2) * hd)
                q2 = q_ref[:, qcols]
                parts = []
                for e in range(2):
                    kk, vv = (k2, v2) if e == sub else (k2r, v2r)
                    qe = jnp.where(head_lanes[e], q2, jnp.zeros_like(q2))
                    sb = lax.dot_general(qe, kk, nt, preferred_element_type=F32)
                    s = jnp.where(use_prev, sb[:, :w], jnp.where(upper, NEG_BIG, sb[:, w:]))
                    sink = sink_ref[head0 + e]
                    m = jnp.maximum(jnp.max(s, axis=-1, keepdims=True), sink)
                    p = jnp.exp(s - m)
                    denom = jnp.sum(p, axis=-1, keepdims=True) + jnp.exp(sink - m)
                    pcat = jnp.concatenate([jnp.where(upper, p, 0.0), jnp.where(upper, 0.0, p)],
                                           axis=1).astype(BF16)
                    parts.append(jnp.dot(pcat, vv, preferred_element_type=F32) * (1.0 / denom))
                o = jnp.where(head_lanes[0], parts[0], parts[1])
                gate = sg_ref[:, qcols].astype(F32)
                o_ref[:, qcols] = (o * (gate * jax.nn.sigmoid(gate))).astype(o_ref.dtype)


def swa(z, sinks, *, batch):
    t = z.shape[0]
    w = SWA_WINDOW
    steps = t // batch // w
    qw, kw = SWA_Q_WIDTH, SWA_KV_WIDTH
    cur = lambda b, i: b * steps + i
    prev = lambda b, i: b * steps + jnp.maximum(i - 1, 0)
    return pl.pallas_call(
        _swa_kernel,
        grid=(batch, steps),
        in_specs=[
            pl.BlockSpec(memory_space=pltpu.SMEM),
            pl.BlockSpec((w, qw), lambda b, i: (cur(b, i), Z_SQ // qw)),
            pl.BlockSpec((w, kw), lambda b, i: (prev(b, i), Z_SK // kw)),
            pl.BlockSpec((w, kw), lambda b, i: (cur(b, i), Z_SK // kw)),
            pl.BlockSpec((w, kw), lambda b, i: (prev(b, i), Z_SV // kw)),
            pl.BlockSpec((w, kw), lambda b, i: (cur(b, i), Z_SV // kw)),
            pl.BlockSpec((w, qw), lambda b, i: (cur(b, i), Z_SG // qw)),
        ],
        out_specs=pl.BlockSpec((w, qw), lambda b, i: (cur(b, i), 0)),
        out_shape=jax.ShapeDtypeStruct((t, qw), BF16),
        compiler_params=_params(("arbitrary", "arbitrary")),
        name="swa",
    )(sinks, z, z, z, z, z, z)


def kernel(x, norm_gains, w_in, b_gates, w_decay_up, b_decay, gla_norm_gains, sinks,
           w_gla_out, w_swa_out, w_out, final_norm_gain):
    batch, seq, d = x.shape
    depth = w_in.shape[0]
    xt = x.reshape(batch * seq, d)
    w_main = pack_w_in(w_in.reshape(depth * d, w_in.shape[2]))
    w_ga = cast_bf16(w_gla_out.reshape(depth * GLA_V_WIDTH, d))
    w_sw = cast_bf16(w_swa_out.reshape(depth * SWA_Q_WIDTH, d))
    w_o = cast_bf16(w_out.reshape(depth * d, d))
    pad_dec = DECAY_PAD - GLA_DECAY_RANK
    for l in range(depth):
        w_dec = jnp.pad(w_in[l, :, W_IN_DECAY:W_IN_SQ], ((0, 0), (0, pad_dec))).astype(BF16)
        w_up = jnp.pad(w_decay_up[l], ((0, pad_dec), (0, 0))).astype(BF16)
        h, la = norm_decay(xt, norm_gains[l][None, :], w_dec, w_up, b_decay[l][None, :])
        z = in_proj(h, w_main, l)
        ua = gla(z, la, gla_norm_gains[l][None, :], batch=batch)
        ub = swa(z, sinks[l], batch=batch)
        merged = merge(ua, ub, w_ga, w_sw, z, b_gates[l], l)
        xt = out_proj(merged, w_o, xt, l)
    out = final_norm(xt, final_norm_gain[None, :])
    return out.reshape(batch, seq, d)
```

```python
import functools

import jax
import jax.numpy as jnp
from jax import lax
from jax.experimental import pallas as pl
from jax.experimental.pallas import tpu as pltpu

F32 = jnp.float32
BF16 = jnp.bfloat16

D_MODEL = 4096
GLA_HEADS = 8
GLA_DK = 128
GLA_DV = 256
GLA_QK_WIDTH = GLA_HEADS * GLA_DK
GLA_V_WIDTH = GLA_HEADS * GLA_DV
GLA_DECAY_RANK = 16
GLA_GATE_NORMALIZER = 16.0
GLA_LOG_DECAY_MIN = -1.0
GLA_CHUNK = 64
SWA_Q_HEADS = 32
SWA_KV_HEADS = 8
SWA_HEAD_DIM = 64
SWA_WINDOW = 128
SWA_Q_WIDTH = SWA_Q_HEADS * SWA_HEAD_DIM
SWA_KV_WIDTH = SWA_KV_HEADS * SWA_HEAD_DIM
RMS_EPS = 1e-6

LANES = 128
DECAY_PAD = LANES

W_IN_DECAY = 2 * GLA_QK_WIDTH + 2 * GLA_V_WIDTH
W_IN_SQ = W_IN_DECAY + GLA_DECAY_RANK
W_IN_SK = W_IN_SQ + SWA_Q_WIDTH
W_IN_SG = W_IN_SK + 2 * SWA_KV_WIDTH
W_IN_GA = W_IN_SG + SWA_Q_WIDTH
Z_GQ = 0
Z_GK = Z_GQ + GLA_QK_WIDTH
Z_GV = Z_GK + GLA_QK_WIDTH
Z_GG = Z_GV + GLA_V_WIDTH
Z_SQ = Z_GG + GLA_V_WIDTH
Z_SG = Z_SQ + SWA_Q_WIDTH
Z_SK = Z_SG + SWA_Q_WIDTH
Z_SV = Z_SK + SWA_KV_WIDTH
Z_GA = Z_SV + SWA_KV_WIDTH
Z_GB = Z_GA + D_MODEL
Z_WIDTH = Z_GB + D_MODEL

VMEM_LIMIT_BYTES = 56 * 1024 * 1024

NEG_BIG = -1e30
LOG2E = 1.4426950408889634
NT_DIMS = (((1,), (1,)), ((), ()))


def _params(semantics):
    return pltpu.CompilerParams(dimension_semantics=semantics,
                                vmem_limit_bytes=VMEM_LIMIT_BYTES)


def _cast_kernel(a_ref, o_ref):
    o_ref[...] = a_ref[...].astype(o_ref.dtype)


def cast_bf16(w, *, tr=512):
    rows, cols = w.shape
    return pl.pallas_call(
        _cast_kernel,
        grid=(rows // tr,),
        in_specs=[pl.BlockSpec((tr, cols), lambda i: (i, 0))],
        out_specs=pl.BlockSpec((tr, cols), lambda i: (i, 0)),
        out_shape=jax.ShapeDtypeStruct((rows, cols), BF16),
        compiler_params=_params(("arbitrary",)),
        name="cast_bf16",
    )(w)


def _rms_rows(x, gain):
    ms = jnp.mean(x * x, axis=-1, keepdims=True)
    return x * lax.rsqrt(ms + RMS_EPS) * gain


def _log_sigmoid(x):
    return jnp.minimum(x, 0.0) - jnp.log(1.0 + jnp.exp(-jnp.abs(x)))


def _norm_decay_kernel(x_ref, g_ref, wdec_ref, wup_ref, bdec_ref, h_ref, la_ref):
    h = _rms_rows(x_ref[...], g_ref[...]).astype(BF16)
    h_ref[...] = h
    gdec = lax.dot_general(h, wdec_ref[...].astype(BF16), NT_DIMS, preferred_element_type=F32)
    logit = jnp.dot(gdec.astype(BF16), wup_ref[...], preferred_element_type=F32) + bdec_ref[...]
    la_ref[...] = jnp.maximum(_log_sigmoid(logit) / GLA_GATE_NORMALIZER, GLA_LOG_DECAY_MIN)


def norm_decay(x, gain, wdec, wup, bdec, *, tm=512):
    t, d = x.shape
    tm = min(tm, t)
    return pl.pallas_call(
        _norm_decay_kernel,
        grid=(t // tm,),
        in_specs=[
            pl.BlockSpec((tm, d), lambda i: (i, 0)),
            pl.BlockSpec((1, d), lambda i: (0, 0)),
            pl.BlockSpec((DECAY_PAD, d), lambda i: (0, 0)),
            pl.BlockSpec((DECAY_PAD, GLA_QK_WIDTH), lambda i: (0, 0)),
            pl.BlockSpec((1, GLA_QK_WIDTH), lambda i: (0, 0)),
        ],
        out_specs=[
            pl.BlockSpec((tm, d), lambda i: (i, 0)),
            pl.BlockSpec((tm, GLA_QK_WIDTH), lambda i: (i, 0)),
        ],
        out_shape=[
            jax.ShapeDtypeStruct((t, d), BF16),
            jax.ShapeDtypeStruct((t, GLA_QK_WIDTH), F32),
        ],
        compiler_params=_params(("arbitrary",)),
        name="norm_decay",
    )(x, gain, wdec, wup, bdec)


def _final_norm_kernel(x_ref, g_ref, o_ref):
    o_ref[...] = _rms_rows(x_ref[...], g_ref[...])


def final_norm(x, gain, *, tm=512):
    t, d = x.shape
    tm = min(tm, t)
    return pl.pallas_call(
        _final_norm_kernel,
        grid=(t // tm,),
        in_specs=[pl.BlockSpec((tm, d), lambda i: (i, 0)),
                  pl.BlockSpec((1, d), lambda i: (0, 0))],
        out_specs=pl.BlockSpec((tm, d), lambda i: (i, 0)),
        out_shape=jax.ShapeDtypeStruct((t, d), F32),
        compiler_params=_params(("arbitrary",)),
        name="final_norm",
    )(x, gain)


def _in_proj_kernel(h_ref, wt_ref, z_ref):
    w = wt_ref[0].astype(BF16)
    z_ref[...] = lax.dot_general(h_ref[...], w, NT_DIMS,
                                 preferred_element_type=F32).astype(z_ref.dtype)


def _w_in_row(j, tn):
    u = GLA_DECAY_RANK
    sg_lo, sk_lo, ga_lo = Z_SG // tn, Z_SK // tn, Z_GA // tn
    col = j * (tn // u)
    units = jnp.where(j < Z_SQ // tn, col,
                      jnp.where(j < sg_lo, col + (W_IN_SQ - Z_SQ) // u,
                                jnp.where(j < sk_lo, col + (W_IN_SG - Z_SG) // u,
                                          jnp.where(j < ga_lo, col + (W_IN_SK - Z_SK) // u,
                                                    col + (W_IN_GA - Z_GA) // u))))
    return units * u


def in_proj(h, wt, layer, *, tm=2048, tn=512):
    t, d = h.shape
    tm = min(tm, t)
    return pl.pallas_call(
        _in_proj_kernel,
        grid=(t // tm, Z_WIDTH // tn),
        in_specs=[pl.BlockSpec((tm, d), lambda i, j: (i, 0), pipeline_mode=pl.Buffered(1)),
                  pl.BlockSpec((pl.Element(1), pl.Element(tn), pl.Element(d)),
                               lambda i, j: (layer, _w_in_row(j, tn), 0))],
        out_specs=pl.BlockSpec((tm, tn), lambda i, j: (i, j)),
        out_shape=jax.ShapeDtypeStruct((t, Z_WIDTH), BF16),
        compiler_params=_params(("arbitrary", "arbitrary")),
        name="in_proj",
    )(h, wt)


def _out_proj_kernel(m_ref, w_ref, x_ref, o_ref):
    o_ref[...] = x_ref[...] + jnp.dot(m_ref[...], w_ref[...], preferred_element_type=F32)


def out_proj(merged, w, x, layer, *, tm=1024, tn=1024):
    t, d = merged.shape
    tm = min(tm, t)
    n = w.shape[1]
    return pl.pallas_call(
        _out_proj_kernel,
        grid=(t // tm, n // tn),
        in_specs=[pl.BlockSpec((tm, d), lambda i, j: (i, 0)),
                  pl.BlockSpec((d, tn), lambda i, j: (layer, j)),
                  pl.BlockSpec((tm, tn), lambda i, j: (i, j))],
        out_specs=pl.BlockSpec((tm, tn), lambda i, j: (i, j)),
        out_shape=jax.ShapeDtypeStruct((t, n), F32),
        compiler_params=_params(("arbitrary", "arbitrary")),
        name="out_proj",
    )(merged, w, x)


def _merge_kernel(ua_ref, ub_ref, wa_ref, wb_ref, ga_ref, gb_ref, bg_ref, o_ref):
    ya = jnp.dot(ua_ref[...], wa_ref[...], preferred_element_type=F32)
    yb = jnp.dot(ub_ref[...], wb_ref[...], preferred_element_type=F32)
    bg = bg_ref[...]
    sa = jax.nn.sigmoid(ga_ref[...].astype(F32) + bg[0:1, :])
    sb = jax.nn.sigmoid(gb_ref[...].astype(F32) + bg[1:2, :])
    o_ref[...] = (sa * ya + sb * yb).astype(o_ref.dtype)


def merge(ua, ub, wa, wb, z, bgates, layer, *, tm=1024, tn=1024):
    t, ka = ua.shape
    tm = min(tm, t)
    kb = ub.shape[1]
    n = wa.shape[1]
    ga_blk = Z_GA // tn
    gb_blk = Z_GB // tn
    return pl.pallas_call(
        _merge_kernel,
        grid=(t // tm, n // tn),
        in_specs=[pl.BlockSpec((tm, ka), lambda i, j: (i, 0)),
                  pl.BlockSpec((tm, kb), lambda i, j: (i, 0)),
                  pl.BlockSpec((ka, tn), lambda i, j: (layer, j)),
                  pl.BlockSpec((kb, tn), lambda i, j: (layer, j)),
                  pl.BlockSpec((tm, tn), lambda i, j: (i, ga_blk + j)),
                  pl.BlockSpec((tm, tn), lambda i, j: (i, gb_blk + j)),
                  pl.BlockSpec((2, tn), lambda i, j: (0, j))],
        out_specs=pl.BlockSpec((tm, tn), lambda i, j: (i, j)),
        out_shape=jax.ShapeDtypeStruct((t, n), BF16),
        compiler_params=_params(("arbitrary", "arbitrary")),
        name="merge",
    )(ua, ub, wa, wb, z, z, bgates)


def _cumsum_rows(x):
    n = x.shape[0]
    row = lax.broadcasted_iota(jnp.int32, x.shape, 0)
    s = 1
    while s < n:
        x = x + jnp.where(row >= s, pltpu.roll(x, s, axis=0), 0.0)
        s *= 2
    return x


def _gla_kernel(q_ref, k_ref, v_ref, gg_ref, la_ref, gain_ref, o_ref, st_ref, *, rows):
    @pl.when(pl.program_id(1) == 0)
    def _():
        st_ref[...] = jnp.zeros_like(st_ref)

    c_sz = GLA_CHUNK
    scale = GLA_DK ** -0.5
    ri = lax.broadcasted_iota(jnp.int32, (c_sz, c_sz), 0)
    ci = lax.broadcasted_iota(jnp.int32, (c_sz, c_sz), 1)
    causal = ci <= ri
    gain = gain_ref[...]
    nt = (((1,), (1,)), ((), ()))
    tn = (((0,), (0,)), ((), ()))

    for h in range(GLA_HEADS):
        kcols = slice(h * GLA_DK, (h + 1) * GLA_DK)
        vcols = slice(h * GLA_DV, (h + 1) * GLA_DV)
        for c in range(rows // c_sz):
            r = slice(c * c_sz, (c + 1) * c_sz)
            qf = q_ref[r, kcols].astype(F32) * scale
            kf = k_ref[r, kcols].astype(F32)
            v = v_ref[r, vcols]
            b = _cumsum_rows(la_ref[r, kcols]) * LOG2E
            b_ref = b[c_sz // 2:c_sz // 2 + 1, :]
            b_last = b[c_sz - 1:c_sz, :]
            q_intra = (qf * jnp.exp2(b - b_ref)).astype(BF16)
            k_intra = (kf * jnp.exp2(b_ref - b)).astype(BF16)
            q_inter = (qf * jnp.exp2(b)).astype(BF16)
            k_state = (kf * jnp.exp2(b_last - b)).astype(BF16)
            decay = jnp.exp2(b_last)

            a = lax.dot_general(q_intra, k_intra, nt, preferred_element_type=F32)
            a = jnp.where(causal, a, 0.0).astype(BF16)
            st = st_ref[h]
            o = jnp.dot(a, v, preferred_element_type=F32)
            o = o + lax.dot_general(q_inter, st.astype(BF16), nt, preferred_element_type=F32)
            st_ref[h] = st * decay + lax.dot_general(v, k_state, tn, preferred_element_type=F32)

            y = _rms_rows(o, gain)
            g = gg_ref[r, vcols].astype(F32)
            o_ref[r, vcols] = (y * (g * jax.nn.sigmoid(g))).astype(o_ref.dtype)


def gla(z, la, gain, *, batch, rows=256):
    t = z.shape[0]
    steps = t // batch // rows
    qw, vw = GLA_QK_WIDTH, GLA_V_WIDTH
    row_map = lambda b, i: b * steps + i
    return pl.pallas_call(
        functools.partial(_gla_kernel, rows=rows),
        grid=(batch, steps),
        in_specs=[
            pl.BlockSpec((rows, qw), lambda b, i: (row_map(b, i), Z_GQ // qw)),
            pl.BlockSpec((rows, qw), lambda b, i: (row_map(b, i), Z_GK // qw)),
            pl.BlockSpec((rows, vw), lambda b, i: (row_map(b, i), Z_GV // vw)),
            pl.BlockSpec((rows, vw), lambda b, i: (row_map(b, i), Z_GG // vw)),
            pl.BlockSpec((rows, qw), lambda b, i: (row_map(b, i), 0)),
            pl.BlockSpec((1, GLA_DV), lambda b, i: (0, 0)),
        ],
        out_specs=pl.BlockSpec((rows, vw), lambda b, i: (row_map(b, i), 0)),
        out_shape=jax.ShapeDtypeStruct((t, vw), BF16),
        scratch_shapes=[pltpu.VMEM((GLA_HEADS, GLA_DV, GLA_DK), F32)],
        compiler_params=_params(("arbitrary", "arbitrary")),
        name="gla",
    )(z, z, z, z, la, gain)


def _swa_kernel(sink_ref, q_ref, kp_ref, kc_ref, vp_ref, vc_ref, sg_ref, o_ref):
    w = SWA_WINDOW
    hd = SWA_HEAD_DIM
    rep = SWA_Q_HEADS // SWA_KV_HEADS
    scale = hd ** -0.5
    first_prev_key = jnp.where(pl.program_id(1) > 0, 0, w)

    qi = lax.broadcasted_iota(jnp.int32, (w, w), 0)
    kj = lax.broadcasted_iota(jnp.int32, (w, w), 1)
    upper = kj > qi
    use_prev = upper & (kj >= first_prev_key)
    lane = lax.broadcasted_iota(jnp.int32, (w, LANES), 1)
    head_lanes = (lane < hd, lane >= hd)
    lane2 = lax.broadcasted_iota(jnp.int32, (2 * w, LANES), 1)
    half_lanes = (lane2 < hd, lane2 >= hd)

    for pair in range(SWA_KV_HEADS // 2):
        cols = slice(pair * LANES, (pair + 1) * LANES)
        k2 = jnp.concatenate([kp_ref[:, cols], kc_ref[:, cols]], axis=0) * scale
        v2 = jnp.concatenate([vp_ref[:, cols], vc_ref[:, cols]], axis=0)
        k2r = pltpu.roll(k2, hd, axis=1)
        v2r = pltpu.roll(v2, hd, axis=1)
        for sub in range(2):
            g = 2 * pair + sub
            kg = jnp.where(half_lanes[sub], k2, k2r)
            vg = jnp.where(half_lanes[sub], v2, v2r)
            qs = []
            for hp in range(rep // 2):
                head0 = g * rep + 2 * hp
                q2 = q_ref[:, head0 * hd:(head0 + 2) * hd]
                for e in range(2):
                    qs.append(jnp.where(head_lanes[e], q2, jnp.zeros_like(q2)))
            sb = lax.dot_general(jnp.concatenate(qs, axis=0), kg, NT_DIMS,
                                 preferred_element_type=F32)
            ps, invs = [], []
            for i in range(rep):
                blk = sb[i * w:(i + 1) * w]
                s = jnp.where(use_prev, blk[:, :w], jnp.where(upper, NEG_BIG, blk[:, w:]))
                sink = sink_ref[g * rep + i]
                m = jnp.maximum(jnp.max(s, axis=-1, keepdims=True), sink)
                p = jnp.exp(s - m)
                denom = jnp.sum(p, axis=-1, keepdims=True) + jnp.exp(sink - m)
                invs.append(1.0 / denom)
                ps.append(jnp.concatenate([jnp.where(upper, p, 0.0), jnp.where(upper, 0.0, p)],
                                          axis=1).astype(BF16))
            o4 = jnp.dot(jnp.concatenate(ps, axis=0), vg, preferred_element_type=F32)
            for hp in range(rep // 2):
                head0 = g * rep + 2 * hp
                qcols = slice(head0 * hd, (head0 + 2) * hd)
                o_a = o4[(2 * hp) * w:(2 * hp + 1) * w] * invs[2 * hp]
                o_b = o4[(2 * hp + 1) * w:(2 * hp + 2) * w] * invs[2 * hp + 1]
                o = jnp.where(head_lanes[0], o_a, o_b)
                gate = sg_ref[:, qcols].astype(F32)
                o_ref[:, qcols] = (o * (gate * jax.nn.sigmoid(gate))).astype(o_ref.dtype)


def swa(z, sinks, *, batch):
    t = z.shape[0]
    w = SWA_WINDOW
    steps = t // batch // w
    qw, kw = SWA_Q_WIDTH, SWA_KV_WIDTH
    cur = lambda b, i: b * steps + i
    prev = lambda b, i: b * steps + jnp.maximum(i - 1, 0)
    return pl.pallas_call(
        _swa_kernel,
        grid=(batch, steps),
        in_specs=[
            pl.BlockSpec(memory_space=pltpu.SMEM),
            pl.BlockSpec((w, qw), lambda b, i: (cur(b, i), Z_SQ // qw)),
            pl.BlockSpec((w, kw), lambda b, i: (prev(b, i), Z_SK // kw)),
            pl.BlockSpec((w, kw), lambda b, i: (cur(b, i), Z_SK // kw)),
            pl.BlockSpec((w, kw), lambda b, i: (prev(b, i), Z_SV // kw)),
            pl.BlockSpec((w, kw), lambda b, i: (cur(b, i), Z_SV // kw)),
            pl.BlockSpec((w, qw), lambda b, i: (cur(b, i), Z_SG // qw)),
        ],
        out_specs=pl.BlockSpec((w, qw), lambda b, i: (cur(b, i), 0)),
        out_shape=jax.ShapeDtypeStruct((t, qw), BF16),
        compiler_params=_params(("arbitrary", "arbitrary")),
        name="swa",
    )(sinks, z, z, z, z, z, z)


def kernel(x, norm_gains, w_in, b_gates, w_decay_up, b_decay, gla_norm_gains, sinks,
           w_gla_out, w_swa_out, w_out, final_norm_gain):
    batch, seq, d = x.shape
    depth = w_in.shape[0]
    xt = x.reshape(batch * seq, d)
    w_in_t = jnp.swapaxes(w_in, 1, 2)
    w_ga = cast_bf16(w_gla_out.reshape(depth * GLA_V_WIDTH, d))
    w_sw = cast_bf16(w_swa_out.reshape(depth * SWA_Q_WIDTH, d))
    w_o = cast_bf16(w_out.reshape(depth * d, d))
    pad_dec = DECAY_PAD - GLA_DECAY_RANK
    for l in range(depth):
        w_dec = jnp.pad(w_in_t[l, W_IN_DECAY:W_IN_SQ, :], ((0, pad_dec), (0, 0)))
        w_up = jnp.pad(w_decay_up[l], ((0, pad_dec), (0, 0))).astype(BF16)
        h, la = norm_decay(xt, norm_gains[l][None, :], w_dec, w_up, b_decay[l][None, :])
        z = in_proj(h, w_in_t, l)
        ua = gla(z, la, gla_norm_gains[l][None, :], batch=batch)
        ub = swa(z, sinks[l], batch=batch)
        merged = merge(ua, ub, w_ga, w_sw, z, b_gates[l], l)
        xt = out_proj(merged, w_o, xt, l)
    out = final_norm(xt, final_norm_gain[None, :])
    return out.reshape(batch, seq, d)
```

```python
import functools

import jax
import jax.numpy as jnp
from jax import lax
from jax.experimental import pallas as pl
from jax.experimental.pallas import tpu as pltpu

F32 = jnp.float32
BF16 = jnp.bfloat16

D_MODEL = 4096
GLA_HEADS = 8
GLA_DK = 128
GLA_DV = 256
GLA_QK_WIDTH = GLA_HEADS * GLA_DK
GLA_V_WIDTH = GLA_HEADS * GLA_DV
GLA_DECAY_RANK = 16
GLA_GATE_NORMALIZER = 16.0
GLA_LOG_DECAY_MIN = -1.0
GLA_CHUNK = 64
GLA_AHEAD = 2
SWA_Q_HEADS = 32
SWA_KV_HEADS = 8
SWA_HEAD_DIM = 64
SWA_WINDOW = 128
SWA_AHEAD = 2
SWA_Q_WIDTH = SWA_Q_HEADS * SWA_HEAD_DIM
SWA_KV_WIDTH = SWA_KV_HEADS * SWA_HEAD_DIM
RMS_EPS = 1e-6

LANES = 128
DECAY_PAD = LANES

W_IN_DECAY = 2 * GLA_QK_WIDTH + 2 * GLA_V_WIDTH
W_IN_SQ = W_IN_DECAY + GLA_DECAY_RANK
W_IN_SK = W_IN_SQ + SWA_Q_WIDTH
W_IN_SG = W_IN_SK + 2 * SWA_KV_WIDTH
W_IN_GA = W_IN_SG + SWA_Q_WIDTH
Z_GQ = 0
Z_GK = Z_GQ + GLA_QK_WIDTH
Z_GV = Z_GK + GLA_QK_WIDTH
Z_GG = Z_GV + GLA_V_WIDTH
Z_SQ = Z_GG + GLA_V_WIDTH
Z_SG = Z_SQ + SWA_Q_WIDTH
Z_SK = Z_SG + SWA_Q_WIDTH
Z_SV = Z_SK + SWA_KV_WIDTH
Z_GA = Z_SV + SWA_KV_WIDTH
Z_GB = Z_GA + D_MODEL
Z_WIDTH = Z_GB + D_MODEL

VMEM_LIMIT_BYTES = 56 * 1024 * 1024

NEG_BIG = -1e30
LOG2E = 1.4426950408889634
NT_DIMS = (((1,), (1,)), ((), ()))


def _params(semantics):
    return pltpu.CompilerParams(dimension_semantics=semantics,
                                vmem_limit_bytes=VMEM_LIMIT_BYTES)


def _cast_kernel(a_ref, o_ref):
    o_ref[...] = a_ref[...].astype(o_ref.dtype)


def cast_bf16(w, *, tr=512):
    rows, cols = w.shape
    return pl.pallas_call(
        _cast_kernel,
        grid=(rows // tr,),
        in_specs=[pl.BlockSpec((tr, cols), lambda i: (i, 0))],
        out_specs=pl.BlockSpec((tr, cols), lambda i: (i, 0)),
        out_shape=jax.ShapeDtypeStruct((rows, cols), BF16),
        compiler_params=_params(("arbitrary",)),
        name="cast_bf16",
    )(w)


def _rms_rows(x, gain):
    ms = jnp.mean(x * x, axis=-1, keepdims=True)
    return x * lax.rsqrt(ms + RMS_EPS) * gain


def _log_sigmoid(x):
    return jnp.minimum(x, 0.0) - jnp.log(1.0 + jnp.exp(-jnp.abs(x)))


NORM_SUB_ROWS = 128


def _norm_decay_kernel(x_ref, g_ref, wdec_ref, wup_ref, bdec_ref, h_ref, la_ref):
    c_sz = GLA_CHUNK
    ri = lax.broadcasted_iota(jnp.int32, (c_sz, c_sz), 0)
    ci = lax.broadcasted_iota(jnp.int32, (c_sz, c_sz), 1)
    tril = (ci <= ri).astype(BF16)
    gain = g_ref[...]
    wdec = wdec_ref[...].astype(BF16)
    for t in range(x_ref.shape[0] // NORM_SUB_ROWS):
        rows = slice(t * NORM_SUB_ROWS, (t + 1) * NORM_SUB_ROWS)
        h = _rms_rows(x_ref[rows, :], gain).astype(BF16)
        h_ref[rows, :] = h
        gdec = lax.dot_general(h, wdec, NT_DIMS, preferred_element_type=F32)
        logit = jnp.dot(gdec.astype(BF16), wup_ref[...], preferred_element_type=F32) + bdec_ref[...]
        la = jnp.maximum(_log_sigmoid(logit) / GLA_GATE_NORMALIZER, GLA_LOG_DECAY_MIN)
        hi = la.astype(BF16)
        r1 = la - hi.astype(F32)
        mid = r1.astype(BF16)
        lo = (r1 - mid.astype(F32)).astype(BF16)
        for c in range(NORM_SUB_ROWS // c_sz):
            r = slice(c * c_sz, (c + 1) * c_sz)
            b = (jnp.dot(tril, hi[r], preferred_element_type=F32)
                 + jnp.dot(tril, mid[r], preferred_element_type=F32)
                 + jnp.dot(tril, lo[r], preferred_element_type=F32))
            la_ref[t * NORM_SUB_ROWS + c * c_sz:t * NORM_SUB_ROWS + (c + 1) * c_sz, :] = b * LOG2E


def norm_decay(x, gain, wdec, wup, bdec, *, tm=512):
    t, d = x.shape
    tm = min(tm, t)
    return pl.pallas_call(
        _norm_decay_kernel,
        grid=(t // tm,),
        in_specs=[
            pl.BlockSpec((tm, d), lambda i: (i, 0)),
            pl.BlockSpec((1, d), lambda i: (0, 0)),
            pl.BlockSpec((DECAY_PAD, d), lambda i: (0, 0)),
            pl.BlockSpec((DECAY_PAD, GLA_QK_WIDTH), lambda i: (0, 0)),
            pl.BlockSpec((1, GLA_QK_WIDTH), lambda i: (0, 0)),
        ],
        out_specs=[
            pl.BlockSpec((tm, d), lambda i: (i, 0)),
            pl.BlockSpec((tm, GLA_QK_WIDTH), lambda i: (i, 0)),
        ],
        out_shape=[
            jax.ShapeDtypeStruct((t, d), BF16),
            jax.ShapeDtypeStruct((t, GLA_QK_WIDTH), F32),
        ],
        compiler_params=_params(("arbitrary",)),
        name="norm_decay",
    )(x, gain, wdec, wup, bdec)


def _final_norm_kernel(x_ref, g_ref, o_ref):
    o_ref[...] = _rms_rows(x_ref[...], g_ref[...])


def final_norm(x, gain, *, tm=512):
    t, d = x.shape
    tm = min(tm, t)
    return pl.pallas_call(
        _final_norm_kernel,
        grid=(t // tm,),
        in_specs=[pl.BlockSpec((tm, d), lambda i: (i, 0)),
                  pl.BlockSpec((1, d), lambda i: (0, 0))],
        out_specs=pl.BlockSpec((tm, d), lambda i: (i, 0)),
        out_shape=jax.ShapeDtypeStruct((t, d), F32),
        compiler_params=_params(("arbitrary",)),
        name="final_norm",
    )(x, gain)


def _in_proj_kernel(h_ref, wt_ref, z_ref):
    w = wt_ref[0].astype(BF16)
    z_ref[...] = lax.dot_general(h_ref[...], w, NT_DIMS,
                                 preferred_element_type=F32).astype(z_ref.dtype)


def _w_in_row(j, tn):
    u = GLA_DECAY_RANK
    sg_lo, sk_lo, ga_lo = Z_SG // tn, Z_SK // tn, Z_GA // tn
    col = j * (tn // u)
    units = jnp.where(j < Z_SQ // tn, col,
                      jnp.where(j < sg_lo, col + (W_IN_SQ - Z_SQ) // u,
                                jnp.where(j < sk_lo, col + (W_IN_SG - Z_SG) // u,
                                          jnp.where(j < ga_lo, col + (W_IN_SK - Z_SK) // u,
                                                    col + (W_IN_GA - Z_GA) // u))))
    return units * u


def in_proj(h, wt, layer, *, tm=2048, tn=512):
    t, d = h.shape
    tm = min(tm, t)
    return pl.pallas_call(
        _in_proj_kernel,
        grid=(t // tm, Z_WIDTH // tn),
        in_specs=[pl.BlockSpec((tm, d), lambda i, j: (i, 0), pipeline_mode=pl.Buffered(1)),
                  pl.BlockSpec((pl.Element(1), pl.Element(tn), pl.Element(d)),
                               lambda i, j: (layer, _w_in_row(j, tn), 0))],
        out_specs=pl.BlockSpec((tm, tn), lambda i, j: (i, j)),
        out_shape=jax.ShapeDtypeStruct((t, Z_WIDTH), BF16),
        compiler_params=_params(("arbitrary", "arbitrary")),
        name="in_proj",
    )(h, wt)


def _out_proj_kernel(m_ref, w_ref, x_ref, o_ref):
    o_ref[...] = x_ref[...] + jnp.dot(m_ref[...], w_ref[...], preferred_element_type=F32)


def out_proj(merged, w, x, layer, *, tm=1024, tn=1024):
    t, d = merged.shape
    tm = min(tm, t)
    n = w.shape[1]
    return pl.pallas_call(
        _out_proj_kernel,
        grid=(t // tm, n // tn),
        in_specs=[pl.BlockSpec((tm, d), lambda i, j: (i, 0)),
                  pl.BlockSpec((d, tn), lambda i, j: (layer, j)),
                  pl.BlockSpec((tm, tn), lambda i, j: (i, j))],
        out_specs=pl.BlockSpec((tm, tn), lambda i, j: (i, j)),
        out_shape=jax.ShapeDtypeStruct((t, n), F32),
        compiler_params=_params(("arbitrary", "arbitrary")),
        name="out_proj",
    )(merged, w, x)


def _merge_kernel(ua_ref, ub_ref, wa_ref, wb_ref, ga_ref, gb_ref, bg_ref, o_ref):
    ya = jnp.dot(ua_ref[...], wa_ref[...], preferred_element_type=F32)
    yb = jnp.dot(ub_ref[...], wb_ref[...], preferred_element_type=F32)
    bg = bg_ref[...]
    sa = jax.nn.sigmoid(ga_ref[...].astype(F32) + bg[0:1, :])
    sb = jax.nn.sigmoid(gb_ref[...].astype(F32) + bg[1:2, :])
    o_ref[...] = (sa * ya + sb * yb).astype(o_ref.dtype)


def merge(ua, ub, wa, wb, z, bgates, layer, *, tm=1024, tn=1024):
    t, ka = ua.shape
    tm = min(tm, t)
    kb = ub.shape[1]
    n = wa.shape[1]
    ga_blk = Z_GA // tn
    gb_blk = Z_GB // tn
    return pl.pallas_call(
        _merge_kernel,
        grid=(t // tm, n // tn),
        in_specs=[pl.BlockSpec((tm, ka), lambda i, j: (i, 0)),
                  pl.BlockSpec((tm, kb), lambda i, j: (i, 0)),
                  pl.BlockSpec((ka, tn), lambda i, j: (layer, j)),
                  pl.BlockSpec((kb, tn), lambda i, j: (layer, j)),
                  pl.BlockSpec((tm, tn), lambda i, j: (i, ga_blk + j)),
                  pl.BlockSpec((tm, tn), lambda i, j: (i, gb_blk + j)),
                  pl.BlockSpec((2, tn), lambda i, j: (0, j))],
        out_specs=pl.BlockSpec((tm, tn), lambda i, j: (i, j)),
        out_shape=jax.ShapeDtypeStruct((t, n), BF16),
        compiler_params=_params(("arbitrary", "arbitrary")),
        name="merge",
    )(ua, ub, wa, wb, z, z, bgates)


def _gla_kernel(q_ref, k_ref, v_ref, gg_ref, b_ref_in, gain_ref, o_ref, st_ref, *, rows):
    @pl.when(pl.program_id(1) == 0)
    def _():
        st_ref[...] = jnp.zeros_like(st_ref)

    c_sz = GLA_CHUNK
    scale = GLA_DK ** -0.5
    ri = lax.broadcasted_iota(jnp.int32, (c_sz, c_sz), 0)
    ci = lax.broadcasted_iota(jnp.int32, (c_sz, c_sz), 1)
    causal = ci <= ri
    gain = gain_ref[...]
    tn = (((0,), (0,)), ((), ()))

    def stage_a(c, h):
        kcols = slice(h * GLA_DK, (h + 1) * GLA_DK)
        vcols = slice(h * GLA_DV, (h + 1) * GLA_DV)
        r = slice(c * c_sz, (c + 1) * c_sz)
        qf = q_ref[r, kcols].astype(F32) * scale
        kf = k_ref[r, kcols].astype(F32)
        v = v_ref[r, vcols]
        b = b_ref_in[r, kcols]
        b_mid = b[c_sz // 2:c_sz // 2 + 1, :]
        b_last = b[c_sz - 1:c_sz, :]
        decay = jnp.exp2(b_last)
        q_dec = qf * jnp.exp2(b)
        k_grow = kf * jnp.exp2(-b)
        q_intra = (q_dec * jnp.exp2(-b_mid)).astype(BF16)
        k_intra = (k_grow * jnp.exp2(b_mid)).astype(BF16)
        q_inter = q_dec.astype(BF16)
        k_state = (k_grow * decay).astype(BF16)

        a = lax.dot_general(q_intra, k_intra, NT_DIMS, preferred_element_type=F32)
        a = jnp.where(causal, a, 0.0).astype(BF16)
        st = st_ref[h]
        o_inter = lax.dot_general(q_inter, st.astype(BF16), NT_DIMS, preferred_element_type=F32)
        st_ref[h] = st * decay + lax.dot_general(v, k_state, tn, preferred_element_type=F32)
        return a, v, o_inter

    def stage_b(c, h, a, v, o_inter):
        vcols = slice(h * GLA_DV, (h + 1) * GLA_DV)
        r = slice(c * c_sz, (c + 1) * c_sz)
        o = jnp.dot(a, v, preferred_element_type=F32) + o_inter
        y = _rms_rows(o, gain)
        g = gg_ref[r, vcols].astype(F32)
        o_ref[r, vcols] = (y * (g * jax.nn.sigmoid(g))).astype(o_ref.dtype)

    work = [(c, h) for c in range(rows // c_sz) for h in range(GLA_HEADS)]
    ahead = GLA_AHEAD
    pending = [stage_a(*work[i]) for i in range(ahead)]
    for i, (c, h) in enumerate(work):
        if i + ahead < len(work):
            pending.append(stage_a(*work[i + ahead]))
        stage_b(c, h, *pending.pop(0))


def gla(z, la, gain, *, batch, rows=256):
    t = z.shape[0]
    steps = t // batch // rows
    qw, vw = GLA_QK_WIDTH, GLA_V_WIDTH
    row_map = lambda b, i: b * steps + i
    return pl.pallas_call(
        functools.partial(_gla_kernel, rows=rows),
        grid=(batch, steps),
        in_specs=[
            pl.BlockSpec((rows, qw), lambda b, i: (row_map(b, i), Z_GQ // qw)),
            pl.BlockSpec((rows, qw), lambda b, i: (row_map(b, i), Z_GK // qw)),
            pl.BlockSpec((rows, vw), lambda b, i: (row_map(b, i), Z_GV // vw)),
            pl.BlockSpec((rows, vw), lambda b, i: (row_map(b, i), Z_GG // vw)),
            pl.BlockSpec((rows, qw), lambda b, i: (row_map(b, i), 0)),
            pl.BlockSpec((1, GLA_DV), lambda b, i: (0, 0)),
        ],
        out_specs=pl.BlockSpec((rows, vw), lambda b, i: (row_map(b, i), 0)),
        out_shape=jax.ShapeDtypeStruct((t, vw), BF16),
        scratch_shapes=[pltpu.VMEM((GLA_HEADS, GLA_DV, GLA_DK), F32)],
        compiler_params=_params(("arbitrary", "arbitrary")),
        name="gla",
    )(z, z, z, z, la, gain)


def _swa_kernel(sink_ref, q_ref, kp_ref, kc_ref, vp_ref, vc_ref, sg_ref, o_ref):
    w = SWA_WINDOW
    hd = SWA_HEAD_DIM
    rep = SWA_Q_HEADS // SWA_KV_HEADS
    scale = hd ** -0.5 * LOG2E
    first_prev_key = jnp.where(pl.program_id(1) > 0, 0, w)

    kj = lax.broadcasted_iota(jnp.int32, (w, w), 0)
    qi = lax.broadcasted_iota(jnp.int32, (w, w), 1)
    upper = kj > qi
    use_prev = upper & (kj >= first_prev_key)
    lane = lax.broadcasted_iota(jnp.int32, (w, LANES), 1)
    head_lanes = (lane < hd, lane >= hd)
    lane2 = lax.broadcasted_iota(jnp.int32, (2 * w, LANES), 1)
    half_lanes = (lane2 < hd, lane2 >= hd)
    tn_dims = (((0,), (0,)), ((), ()))

    def scores(g):
        pair, sub = divmod(g, 2)
        cols = slice(pair * LANES, (pair + 1) * LANES)
        k2 = jnp.concatenate([kp_ref[:, cols], kc_ref[:, cols]], axis=0) * scale
        kg = jnp.where(half_lanes[sub], k2, pltpu.roll(k2, hd, axis=1))
        qs = []
        for hp in range(rep // 2):
            head0 = g * rep + 2 * hp
            q2 = q_ref[:, head0 * hd:(head0 + 2) * hd]
            for e in range(2):
                qs.append(jnp.where(head_lanes[e], q2, jnp.zeros_like(q2)))
        return lax.dot_general(kg, jnp.concatenate(qs, axis=0), NT_DIMS,
                               preferred_element_type=F32)

    pending = [scores(g) for g in range(SWA_AHEAD)]
    for g in range(SWA_KV_HEADS):
        st = pending.pop(0)
        if g + SWA_AHEAD < SWA_KV_HEADS:
            pending.append(scores(g + SWA_AHEAD))
        pair, sub = divmod(g, 2)
        cols = slice(pair * LANES, (pair + 1) * LANES)
        v2 = jnp.concatenate([vp_ref[:, cols], vc_ref[:, cols]], axis=0)
        ps, invs = [], []
        for i in range(rep):
            blk = st[:, i * w:(i + 1) * w]
            s = jnp.where(use_prev, blk[:w], jnp.where(upper, NEG_BIG, blk[w:]))
            sink = sink_ref[g * rep + i] * LOG2E
            m = jnp.maximum(jnp.max(s, axis=0, keepdims=True), sink)
            p = jnp.exp2(s - m)
            denom = jnp.sum(p, axis=0, keepdims=True) + jnp.exp2(sink - m)
            invs.append(1.0 / denom)
            ps.append(jnp.concatenate([jnp.where(upper, p, 0.0), jnp.where(upper, 0.0, p)],
                                      axis=0).astype(BF16))
        ot = lax.dot_general(v2, jnp.concatenate(ps, axis=1), tn_dims,
                             preferred_element_type=F32)
        drows = slice(sub * hd, (sub + 1) * hd)
        for hp in range(rep // 2):
            head0 = g * rep + 2 * hp
            qcols = slice(head0 * hd, (head0 + 2) * hd)
            o_a = ot[drows, (2 * hp) * w:(2 * hp + 1) * w] * invs[2 * hp]
            o_b = ot[drows, (2 * hp + 1) * w:(2 * hp + 2) * w] * invs[2 * hp + 1]
            o = jnp.concatenate([o_a, o_b], axis=0).T
            gate = sg_ref[:, qcols].astype(F32)
            o_ref[:, qcols] = (o * (gate * jax.nn.sigmoid(gate))).astype(o_ref.dtype)


def swa(z, sinks, *, batch):
    t = z.shape[0]
    w = SWA_WINDOW
    steps = t // batch // w
    qw, kw = SWA_Q_WIDTH, SWA_KV_WIDTH
    cur = lambda b, i: b * steps + i
    prev = lambda b, i: b * steps + jnp.maximum(i - 1, 0)
    return pl.pallas_call(
        _swa_kernel,
        grid=(batch, steps),
        in_specs=[
            pl.BlockSpec(memory_space=pltpu.SMEM),
            pl.BlockSpec((w, qw), lambda b, i: (cur(b, i), Z_SQ // qw)),
            pl.BlockSpec((w, kw), lambda b, i: (prev(b, i), Z_SK // kw)),
            pl.BlockSpec((w, kw), lambda b, i: (cur(b, i), Z_SK // kw)),
            pl.BlockSpec((w, kw), lambda b, i: (prev(b, i), Z_SV // kw)),
            pl.BlockSpec((w, kw), lambda b, i: (cur(b, i), Z_SV // kw)),
            pl.BlockSpec((w, qw), lambda b, i: (cur(b, i), Z_SG // qw)),
        ],
        out_specs=pl.BlockSpec((w, qw), lambda b, i: (cur(b, i), 0)),
        out_shape=jax.ShapeDtypeStruct((t, qw), BF16),
        compiler_params=_params(("arbitrary", "arbitrary")),
        name="swa",
    )(sinks, z, z, z, z, z, z)


def kernel(x, norm_gains, w_in, b_gates, w_decay_up, b_decay, gla_norm_gains, sinks,
           w_gla_out, w_swa_out, w_out, final_norm_gain):
    batch, seq, d = x.shape
    depth = w_in.shape[0]
    xt = x.reshape(batch * seq, d)
    w_in_t = jnp.swapaxes(w_in, 1, 2)
    w_ga = cast_bf16(w_gla_out.reshape(depth * GLA_V_WIDTH, d))
    w_sw = cast_bf16(w_swa_out.reshape(depth * SWA_Q_WIDTH, d))
    w_o = cast_bf16(w_out.reshape(depth * d, d))
    pad_dec = DECAY_PAD - GLA_DECAY_RANK
    for l in range(depth):
        w_dec = jnp.pad(w_in_t[l, W_IN_DECAY:W_IN_SQ, :], ((0, pad_dec), (0, 0)))
        w_up = jnp.pad(w_decay_up[l], ((0, pad_dec), (0, 0))).astype(BF16)
        h, la = norm_decay(xt, norm_gains[l][None, :], w_dec, w_up, b_decay[l][None, :])
        z = in_proj(h, w_in_t, l)
        ua = gla(z, la, gla_norm_gains[l][None, :], batch=batch)
        ub = swa(z, sinks[l], batch=batch)
        merged = merge(ua, ub, w_ga, w_sw, z, b_gates[l], l)
        xt = out_proj(merged, w_o, xt, l)
    out = final_norm(xt, final_norm_gain[None, :])
    return out.reshape(batch, seq, d)
```

```python
import functools

import jax
import jax.numpy as jnp
from jax import lax
from jax.experimental import pallas as pl
from jax.experimental.pallas import tpu as pltpu

F32 = jnp.float32
BF16 = jnp.bfloat16

D_MODEL = 4096
GLA_HEADS = 8
GLA_DK = 128
GLA_DV = 256
GLA_QK_WIDTH = GLA_HEADS * GLA_DK
GLA_V_WIDTH = GLA_HEADS * GLA_DV
GLA_DECAY_RANK = 16
GLA_GATE_NORMALIZER = 16.0
GLA_LOG_DECAY_MIN = -1.0
GLA_CHUNK = 64
GLA_AHEAD = 2
SWA_Q_HEADS = 32
SWA_KV_HEADS = 8
SWA_HEAD_DIM = 64
SWA_WINDOW = 128
SWA_AHEAD = 2
SWA_Q_WIDTH = SWA_Q_HEADS * SWA_HEAD_DIM
SWA_KV_WIDTH = SWA_KV_HEADS * SWA_HEAD_DIM
RMS_EPS = 1e-6

LANES = 128
DECAY_PAD = LANES

W_IN_DECAY = 2 * GLA_QK_WIDTH + 2 * GLA_V_WIDTH
W_IN_SQ = W_IN_DECAY + GLA_DECAY_RANK
W_IN_SK = W_IN_SQ + SWA_Q_WIDTH
W_IN_SG = W_IN_SK + 2 * SWA_KV_WIDTH
W_IN_GA = W_IN_SG + SWA_Q_WIDTH
Z_GQ = 0
Z_GK = Z_GQ + GLA_QK_WIDTH
Z_GV = Z_GK + GLA_QK_WIDTH
Z_GG = Z_GV + GLA_V_WIDTH
Z_SQ = Z_GG + GLA_V_WIDTH
Z_SG = Z_SQ + SWA_Q_WIDTH
Z_SK = Z_SG + SWA_Q_WIDTH
Z_SV = Z_SK + SWA_KV_WIDTH
Z_GA = Z_SV + SWA_KV_WIDTH
Z_GB = Z_GA + D_MODEL
Z_WIDTH = Z_GB + D_MODEL

VMEM_LIMIT_BYTES = 56 * 1024 * 1024

NEG_BIG = -1e30
LOG2E = 1.4426950408889634
NT_DIMS = (((1,), (1,)), ((), ()))


def _params(semantics):
    return pltpu.CompilerParams(dimension_semantics=semantics,
                                vmem_limit_bytes=VMEM_LIMIT_BYTES)


def _rms_rows(x, gain):
    ms = jnp.mean(x * x, axis=-1, keepdims=True)
    return x * lax.rsqrt(ms + RMS_EPS) * gain


def _log_sigmoid(x):
    return jnp.minimum(x, 0.0) - jnp.log(1.0 + jnp.exp(-jnp.abs(x)))


NORM_SUB_ROWS = 128


def _norm_decay_kernel(x_ref, g_ref, wdec_ref, wup_ref, bdec_ref, h_ref, la_ref):
    c_sz = GLA_CHUNK
    ri = lax.broadcasted_iota(jnp.int32, (c_sz, c_sz), 0)
    ci = lax.broadcasted_iota(jnp.int32, (c_sz, c_sz), 1)
    tril = (ci <= ri).astype(BF16)
    gain = g_ref[...]
    wdec = wdec_ref[...].astype(BF16)
    for t in range(x_ref.shape[0] // NORM_SUB_ROWS):
        rows = slice(t * NORM_SUB_ROWS, (t + 1) * NORM_SUB_ROWS)
        h = _rms_rows(x_ref[rows, :], gain).astype(BF16)
        h_ref[rows, :] = h
        gdec = lax.dot_general(h, wdec, NT_DIMS, preferred_element_type=F32)
        logit = jnp.dot(gdec.astype(BF16), wup_ref[...], preferred_element_type=F32) + bdec_ref[...]
        la = jnp.maximum(_log_sigmoid(logit) / GLA_GATE_NORMALIZER, GLA_LOG_DECAY_MIN)
        hi = la.astype(BF16)
        r1 = la - hi.astype(F32)
        mid = r1.astype(BF16)
        lo = (r1 - mid.astype(F32)).astype(BF16)
        for c in range(NORM_SUB_ROWS // c_sz):
            r = slice(c * c_sz, (c + 1) * c_sz)
            b = (jnp.dot(tril, hi[r], preferred_element_type=F32)
                 + jnp.dot(tril, mid[r], preferred_element_type=F32)
                 + jnp.dot(tril, lo[r], preferred_element_type=F32))
            la_ref[t * NORM_SUB_ROWS + c * c_sz:t * NORM_SUB_ROWS + (c + 1) * c_sz, :] = b * LOG2E


def norm_decay(x, gain, wdec, wup, bdec, *, tm=512):
    t, d = x.shape
    tm = min(tm, t)
    return pl.pallas_call(
        _norm_decay_kernel,
        grid=(t // tm,),
        in_specs=[
            pl.BlockSpec((tm, d), lambda i: (i, 0)),
            pl.BlockSpec((1, d), lambda i: (0, 0)),
            pl.BlockSpec((DECAY_PAD, d), lambda i: (0, 0)),
            pl.BlockSpec((DECAY_PAD, GLA_QK_WIDTH), lambda i: (0, 0)),
            pl.BlockSpec((1, GLA_QK_WIDTH), lambda i: (0, 0)),
        ],
        out_specs=[
            pl.BlockSpec((tm, d), lambda i: (i, 0)),
            pl.BlockSpec((tm, GLA_QK_WIDTH), lambda i: (i, 0)),
        ],
        out_shape=[
            jax.ShapeDtypeStruct((t, d), BF16),
            jax.ShapeDtypeStruct((t, GLA_QK_WIDTH), F32),
        ],
        compiler_params=_params(("arbitrary",)),
        name="norm_decay",
    )(x, gain, wdec, wup, bdec)


def _final_norm_kernel(x_ref, g_ref, o_ref):
    o_ref[...] = _rms_rows(x_ref[...], g_ref[...])


def final_norm(x, gain, *, tm=512):
    t, d = x.shape
    tm = min(tm, t)
    return pl.pallas_call(
        _final_norm_kernel,
        grid=(t // tm,),
        in_specs=[pl.BlockSpec((tm, d), lambda i: (i, 0)),
                  pl.BlockSpec((1, d), lambda i: (0, 0))],
        out_specs=pl.BlockSpec((tm, d), lambda i: (i, 0)),
        out_shape=jax.ShapeDtypeStruct((t, d), F32),
        compiler_params=_params(("arbitrary",)),
        name="final_norm",
    )(x, gain)


def _in_proj_kernel(h_ref, wt_ref, z_ref):
    w = wt_ref[0].astype(BF16)
    z_ref[...] = lax.dot_general(h_ref[...], w, NT_DIMS,
                                 preferred_element_type=F32).astype(z_ref.dtype)


def _w_in_row(j, tn):
    u = GLA_DECAY_RANK
    sg_lo, sk_lo, ga_lo = Z_SG // tn, Z_SK // tn, Z_GA // tn
    col = j * (tn // u)
    units = jnp.where(j < Z_SQ // tn, col,
                      jnp.where(j < sg_lo, col + (W_IN_SQ - Z_SQ) // u,
                                jnp.where(j < sk_lo, col + (W_IN_SG - Z_SG) // u,
                                          jnp.where(j < ga_lo, col + (W_IN_SK - Z_SK) // u,
                                                    col + (W_IN_GA - Z_GA) // u))))
    return units * u


def _in_proj_cast_kernel(h_ref, wt_ref, *refs):
    n = (len(refs) - 1) // 2
    _in_proj_kernel(h_ref, wt_ref, refs[n])
    for src, dst in zip(refs[:n], refs[n + 1:]):
        dst[...] = src[...].astype(dst.dtype)


def _cast_rows_per_step(rows, n_steps):
    sub = 16
    r = sub
    while rows % r or rows // r > n_steps:
        r += sub
    return r


def in_proj(h, wt, layer, *, tm=2048, tn=512, cast=()):
    t, d = h.shape
    tm = min(tm, t)
    n_j = Z_WIDTH // tn
    n_steps = (t // tm) * n_j
    in_specs = [
        pl.BlockSpec((tm, d), lambda i, j: (i, 0), pipeline_mode=pl.Buffered(1)),
        pl.BlockSpec((pl.Element(1), pl.Element(tn), pl.Element(d)),
                     lambda i, j: (layer, _w_in_row(j, tn), 0))]
    out_specs = [pl.BlockSpec((tm, tn), lambda i, j: (i, j))]
    out_shape = [jax.ShapeDtypeStruct((t, Z_WIDTH), BF16)]
    cast_specs = []
    for w in cast:
        rows, cols = w.shape
        r = _cast_rows_per_step(rows, n_steps)
        last = rows // r - 1
        spec = pl.BlockSpec((r, cols), lambda i, j, last=last: (jnp.minimum(i * n_j + j, last), 0))
        cast_specs.append(spec)
        out_shape.append(jax.ShapeDtypeStruct((rows, cols), BF16))
    outs = pl.pallas_call(
        _in_proj_cast_kernel if cast else _in_proj_kernel,
        grid=(t // tm, n_j),
        in_specs=in_specs + cast_specs,
        out_specs=out_specs + cast_specs if cast else out_specs[0],
        out_shape=out_shape if cast else out_shape[0],
        compiler_params=_params(("arbitrary", "arbitrary")),
        name="in_proj",
    )(h, wt, *cast)
    return outs


def _out_proj_kernel(m_ref, w_ref, x_ref, o_ref):
    o_ref[...] = x_ref[...] + jnp.dot(m_ref[...], w_ref[...], preferred_element_type=F32)


def out_proj(merged, w, x, layer, *, tm=1024, tn=1024):
    t, d = merged.shape
    tm = min(tm, t)
    n = w.shape[1]
    return pl.pallas_call(
        _out_proj_kernel,
        grid=(t // tm, n // tn),
        in_specs=[pl.BlockSpec((tm, d), lambda i, j: (i, 0)),
                  pl.BlockSpec((d, tn), lambda i, j: (layer, j)),
                  pl.BlockSpec((tm, tn), lambda i, j: (i, j))],
        out_specs=pl.BlockSpec((tm, tn), lambda i, j: (i, j)),
        out_shape=jax.ShapeDtypeStruct((t, n), F32),
        compiler_params=_params(("arbitrary", "arbitrary")),
        name="out_proj",
    )(merged, w, x)


def _merge_kernel(ua_ref, ub_ref, wa_ref, wb_ref, ga_ref, gb_ref, bg_ref, o_ref):
    ya = jnp.dot(ua_ref[...], wa_ref[...], preferred_element_type=F32)
    yb = jnp.dot(ub_ref[...], wb_ref[...], preferred_element_type=F32)
    bg = bg_ref[...]
    sa = jax.nn.sigmoid(ga_ref[...].astype(F32) + bg[0:1, :])
    sb = jax.nn.sigmoid(gb_ref[...].astype(F32) + bg[1:2, :])
    o_ref[...] = (sa * ya + sb * yb).astype(o_ref.dtype)


def merge(ua, ub, wa, wb, z, bgates, layer, *, tm=1024, tn=1024):
    t, ka = ua.shape
    tm = min(tm, t)
    kb = ub.shape[1]
    n = wa.shape[1]
    ga_blk = Z_GA // tn
    gb_blk = Z_GB // tn
    return pl.pallas_call(
        _merge_kernel,
        grid=(t // tm, n // tn),
        in_specs=[pl.BlockSpec((tm, ka), lambda i, j: (i, 0)),
                  pl.BlockSpec((tm, kb), lambda i, j: (i, 0)),
                  pl.BlockSpec((ka, tn), lambda i, j: (layer, j)),
                  pl.BlockSpec((kb, tn), lambda i, j: (layer, j)),
                  pl.BlockSpec((tm, tn), lambda i, j: (i, ga_blk + j)),
                  pl.BlockSpec((tm, tn), lambda i, j: (i, gb_blk + j)),
                  pl.BlockSpec((2, tn), lambda i, j: (0, j))],
        out_specs=pl.BlockSpec((tm, tn), lambda i, j: (i, j)),
        out_shape=jax.ShapeDtypeStruct((t, n), BF16),
        compiler_params=_params(("arbitrary", "arbitrary")),
        name="merge",
    )(ua, ub, wa, wb, z, z, bgates)


def _gla_kernel(q_ref, k_ref, v_ref, gg_ref, b_ref_in, gain_ref, o_ref, st_ref, *, rows):
    @pl.when(pl.program_id(1) == 0)
    def _():
        st_ref[...] = jnp.zeros_like(st_ref)

    c_sz = GLA_CHUNK
    scale = GLA_DK ** -0.5
    ri = lax.broadcasted_iota(jnp.int32, (c_sz, c_sz), 0)
    ci = lax.broadcasted_iota(jnp.int32, (c_sz, c_sz), 1)
    causal = ci <= ri
    gain = gain_ref[...]
    tn = (((0,), (0,)), ((), ()))

    def stage_a(c, h):
        kcols = slice(h * GLA_DK, (h + 1) * GLA_DK)
        vcols = slice(h * GLA_DV, (h + 1) * GLA_DV)
        r = slice(c * c_sz, (c + 1) * c_sz)
        qf = q_ref[r, kcols].astype(F32) * scale
        kf = k_ref[r, kcols].astype(F32)
        v = v_ref[r, vcols]
        b = b_ref_in[r, kcols]
        b_mid = b[c_sz // 2:c_sz // 2 + 1, :]
        b_last = b[c_sz - 1:c_sz, :]
        decay = jnp.exp2(b_last)
        q_dec = qf * jnp.exp2(b)
        k_grow = kf * jnp.exp2(-b)
        q_intra = (q_dec * jnp.exp2(-b_mid)).astype(BF16)
        k_intra = (k_grow * jnp.exp2(b_mid)).astype(BF16)
        q_inter = q_dec.astype(BF16)
        k_state = (k_grow * decay).astype(BF16)

        a = lax.dot_general(q_intra, k_intra, NT_DIMS, preferred_element_type=F32)
        a = jnp.where(causal, a, 0.0).astype(BF16)
        st = st_ref[h]
        o_inter = lax.dot_general(q_inter, st.astype(BF16), NT_DIMS, preferred_element_type=F32)
        st_ref[h] = st * decay + lax.dot_general(v, k_state, tn, preferred_element_type=F32)
        return a, v, o_inter

    def stage_b(c, h, a, v, o_inter):
        vcols = slice(h * GLA_DV, (h + 1) * GLA_DV)
        r = slice(c * c_sz, (c + 1) * c_sz)
        o = jnp.dot(a, v, preferred_element_type=F32) + o_inter
        y = _rms_rows(o, gain)
        g = gg_ref[r, vcols].astype(F32)
        o_ref[r, vcols] = (y * (g * jax.nn.sigmoid(g))).astype(o_ref.dtype)

    work = [(c, h) for c in range(rows // c_sz) for h in range(GLA_HEADS)]
    ahead = GLA_AHEAD
    pending = [stage_a(*work[i]) for i in range(ahead)]
    for i, (c, h) in enumerate(work):
        if i + ahead < len(work):
            pending.append(stage_a(*work[i + ahead]))
        stage_b(c, h, *pending.pop(0))


def gla(z, la, gain, *, batch, rows=512):
    t = z.shape[0]
    steps = t // batch // rows
    qw, vw = GLA_QK_WIDTH, GLA_V_WIDTH
    row_map = lambda b, i: b * steps + i
    return pl.pallas_call(
        functools.partial(_gla_kernel, rows=rows),
        grid=(batch, steps),
        in_specs=[
            pl.BlockSpec((rows, qw), lambda b, i: (row_map(b, i), Z_GQ // qw)),
            pl.BlockSpec((rows, qw), lambda b, i: (row_map(b, i), Z_GK // qw)),
            pl.BlockSpec((rows, vw), lambda b, i: (row_map(b, i), Z_GV // vw)),
            pl.BlockSpec((rows, vw), lambda b, i: (row_map(b, i), Z_GG // vw)),
            pl.BlockSpec((rows, qw), lambda b, i: (row_map(b, i), 0)),
            pl.BlockSpec((1, GLA_DV), lambda b, i: (0, 0)),
        ],
        out_specs=pl.BlockSpec((rows, vw), lambda b, i: (row_map(b, i), 0)),
        out_shape=jax.ShapeDtypeStruct((t, vw), BF16),
        scratch_shapes=[pltpu.VMEM((GLA_HEADS, GLA_DV, GLA_DK), F32)],
        compiler_params=_params(("arbitrary", "arbitrary")),
        name="gla",
    )(z, z, z, z, la, gain)


def _swa_kernel(sink_ref, q_ref, kp_ref, kc_ref, vp_ref, vc_ref, sg_ref, o_ref, *, blocks):
    w = SWA_WINDOW
    hd = SWA_HEAD_DIM
    rep = SWA_Q_HEADS // SWA_KV_HEADS
    scale = hd ** -0.5 * LOG2E
    first_prev_key = jnp.where(pl.program_id(1) > 0, 0, w)

    kj = lax.broadcasted_iota(jnp.int32, (w, w), 0)
    qi = lax.broadcasted_iota(jnp.int32, (w, w), 1)
    upper = kj > qi
    use_prev_first = upper & (kj >= first_prev_key)
    lane = lax.broadcasted_iota(jnp.int32, (w, LANES), 1)
    head_lanes = (lane < hd, lane >= hd)
    lane2 = lax.broadcasted_iota(jnp.int32, (2 * w, LANES), 1)
    half_lanes = (lane2 < hd, lane2 >= hd)
    tn_dims = (((0,), (0,)), ((), ()))

    def band(prev_ref, cur_ref, blk, cols):
        cur = cur_ref[blk * w:(blk + 1) * w, cols]
        prev = prev_ref[:, cols] if blk == 0 else cur_ref[(blk - 1) * w:blk * w, cols]
        return jnp.concatenate([prev, cur], axis=0)

    def scores(blk, g):
        pair, sub = divmod(g, 2)
        cols = slice(pair * LANES, (pair + 1) * LANES)
        k2 = band(kp_ref, kc_ref, blk, cols) * scale
        kg = jnp.where(half_lanes[sub], k2, pltpu.roll(k2, hd, axis=1))
        qs = []
        for hp in range(rep // 2):
            head0 = g * rep + 2 * hp
            q2 = q_ref[blk * w:(blk + 1) * w, head0 * hd:(head0 + 2) * hd]
            for e in range(2):
                qs.append(jnp.where(head_lanes[e], q2, jnp.zeros_like(q2)))
        return lax.dot_general(kg, jnp.concatenate(qs, axis=0), NT_DIMS,
                               preferred_element_type=F32)

    work = [(blk, g) for blk in range(blocks) for g in range(SWA_KV_HEADS)]
    pending = [scores(*work[i]) for i in range(SWA_AHEAD)]
    for n, (blk, g) in enumerate(work):
        st = pending.pop(0)
        if n + SWA_AHEAD < len(work):
            pending.append(scores(*work[n + SWA_AHEAD]))
        pair, sub = divmod(g, 2)
        cols = slice(pair * LANES, (pair + 1) * LANES)
        qrows = slice(blk * w, (blk + 1) * w)
        v2 = band(vp_ref, vc_ref, blk, cols)
        use_prev = use_prev_first if blk == 0 else upper
        ps, invs = [], []
        for i in range(rep):
            sblk = st[:, i * w:(i + 1) * w]
            s = jnp.where(use_prev, sblk[:w], jnp.where(upper, NEG_BIG, sblk[w:]))
            sink = sink_ref[g * rep + i] * LOG2E
            m = jnp.maximum(jnp.max(s, axis=0, keepdims=True), sink)
            p = jnp.exp2(s - m)
            denom = jnp.sum(p, axis=0, keepdims=True) + jnp.exp2(sink - m)
            invs.append(1.0 / denom)
            ps.append(jnp.concatenate([jnp.where(upper, p, 0.0), jnp.where(upper, 0.0, p)],
                                      axis=0).astype(BF16))
        ot = lax.dot_general(v2, jnp.concatenate(ps, axis=1), tn_dims,
                             preferred_element_type=F32)
        drows = slice(sub * hd, (sub + 1) * hd)
        for hp in range(rep // 2):
            head0 = g * rep + 2 * hp
            qcols = slice(head0 * hd, (head0 + 2) * hd)
            o_a = ot[drows, (2 * hp) * w:(2 * hp + 1) * w] * invs[2 * hp]
            o_b = ot[drows, (2 * hp + 1) * w:(2 * hp + 2) * w] * invs[2 * hp + 1]
            o = jnp.concatenate([o_a, o_b], axis=0).T
            gate = sg_ref[qrows, qcols].astype(F32)
            o_ref[qrows, qcols] = (o * (gate * jax.nn.sigmoid(gate))).astype(o_ref.dtype)


def swa(z, sinks, *, batch, blocks=2):
    t = z.shape[0]
    w = SWA_WINDOW
    rows = blocks * w
    steps = t // batch // rows
    qw, kw = SWA_Q_WIDTH, SWA_KV_WIDTH
    cur = lambda b, i: b * steps + i
    prev = lambda b, i: (b * steps + i) * blocks - jnp.where(i > 0, 1, 0)
    return pl.pallas_call(
        functools.partial(_swa_kernel, blocks=blocks),
        grid=(batch, steps),
        in_specs=[
            pl.BlockSpec(memory_space=pltpu.SMEM),
            pl.BlockSpec((rows, qw), lambda b, i: (cur(b, i), Z_SQ // qw)),
            pl.BlockSpec((w, kw), lambda b, i: (prev(b, i), Z_SK // kw)),
            pl.BlockSpec((rows, kw), lambda b, i: (cur(b, i), Z_SK // kw)),
            pl.BlockSpec((w, kw), lambda b, i: (prev(b, i), Z_SV // kw)),
            pl.BlockSpec((rows, kw), lambda b, i: (cur(b, i), Z_SV // kw)),
            pl.BlockSpec((rows, qw), lambda b, i: (cur(b, i), Z_SG // qw)),
        ],
        out_specs=pl.BlockSpec((rows, qw), lambda b, i: (cur(b, i), 0)),
        out_shape=jax.ShapeDtypeStruct((t, qw), BF16),
        compiler_params=_params(("arbitrary", "arbitrary")),
        name="swa",
    )(sinks, z, z, z, z, z, z)


def kernel(x, norm_gains, w_in, b_gates, w_decay_up, b_decay, gla_norm_gains, sinks,
           w_gla_out, w_swa_out, w_out, final_norm_gain):
    batch, seq, d = x.shape
    depth = w_in.shape[0]
    xt = x.reshape(batch * seq, d)
    w_in_t = jnp.swapaxes(w_in, 1, 2)
    to_cast = (w_gla_out.reshape(depth * GLA_V_WIDTH, d), w_swa_out.reshape(depth * SWA_Q_WIDTH, d),
               w_out.reshape(depth * d, d))
    pad_dec = DECAY_PAD - GLA_DECAY_RANK
    for l in range(depth):
        w_dec = jnp.pad(w_in_t[l, W_IN_DECAY:W_IN_SQ, :], ((0, pad_dec), (0, 0)))
        w_up = jnp.pad(w_decay_up[l], ((0, pad_dec), (0, 0))).astype(BF16)
        h, la = norm_decay(xt, norm_gains[l][None, :], w_dec, w_up, b_decay[l][None, :])
        if l == 0:
            z, w_ga, w_sw, w_o = in_proj(h, w_in_t, l, cast=to_cast)
        else:
            z = in_proj(h, w_in_t, l)
        ua = gla(z, la, gla_norm_gains[l][None, :], batch=batch)
        ub = swa(z, sinks[l], batch=batch)
        merged = merge(ua, ub, w_ga, w_sw, z, b_gates[l], l)
        xt = out_proj(merged, w_o, xt, l)
    out = final_norm(xt, final_norm_gain[None, :])
    return out.reshape(batch, seq, d)
```

```python
import functools

import jax
import jax.numpy as jnp
from jax import lax
from jax.experimental import pallas as pl
from jax.experimental.pallas import tpu as pltpu

F32 = jnp.float32
BF16 = jnp.bfloat16

D_MODEL = 4096
GLA_HEADS = 8
GLA_DK = 128
GLA_DV = 256
GLA_QK_WIDTH = GLA_HEADS * GLA_DK
GLA_V_WIDTH = GLA_HEADS * GLA_DV
GLA_DECAY_RANK = 16
GLA_GATE_NORMALIZER = 16.0
GLA_LOG_DECAY_MIN = -1.0
GLA_CHUNK = 64
GLA_AHEAD = 2
SWA_Q_HEADS = 32
SWA_KV_HEADS = 8
SWA_HEAD_DIM = 64
SWA_WINDOW = 128
SWA_AHEAD = 2
SWA_Q_WIDTH = SWA_Q_HEADS * SWA_HEAD_DIM
SWA_KV_WIDTH = SWA_KV_HEADS * SWA_HEAD_DIM
RMS_EPS = 1e-6

LANES = 128
DECAY_PAD = LANES

W_IN_DECAY = 2 * GLA_QK_WIDTH + 2 * GLA_V_WIDTH
W_IN_SQ = W_IN_DECAY + GLA_DECAY_RANK
W_IN_SK = W_IN_SQ + SWA_Q_WIDTH
W_IN_SG = W_IN_SK + 2 * SWA_KV_WIDTH
W_IN_GA = W_IN_SG + SWA_Q_WIDTH
Z_GQ = 0
Z_GK = Z_GQ + GLA_QK_WIDTH
Z_GV = Z_GK + GLA_QK_WIDTH
Z_GG = Z_GV + GLA_V_WIDTH
Z_SQ = Z_GG + GLA_V_WIDTH
Z_SG = Z_SQ + SWA_Q_WIDTH
Z_SK = Z_SG + SWA_Q_WIDTH
Z_SV = Z_SK + SWA_KV_WIDTH
Z_GA = Z_SV + SWA_KV_WIDTH
Z_GB = Z_GA + D_MODEL
Z_WIDTH = Z_GB + D_MODEL

VMEM_LIMIT_BYTES = 56 * 1024 * 1024
IN_PROJ_VMEM_LIMIT_BYTES = 61 * 1024 * 1024

NEG_BIG = -1e30
LOG2E = 1.4426950408889634
NT_DIMS = (((1,), (1,)), ((), ()))


def _params(semantics, vmem_limit_bytes=VMEM_LIMIT_BYTES):
    return pltpu.CompilerParams(dimension_semantics=semantics,
                                vmem_limit_bytes=vmem_limit_bytes)


def _rms_rows(x, gain):
    ms = jnp.mean(x * x, axis=-1, keepdims=True)
    return x * lax.rsqrt(ms + RMS_EPS) * gain


def _log_sigmoid(x):
    return jnp.minimum(x, 0.0) - jnp.log(1.0 + jnp.exp(-jnp.abs(x)))


NORM_SUB_ROWS = 128


def _norm_decay_kernel(x_ref, g_ref, wdec_ref, wup_ref, bdec_ref, h_ref, la_ref):
    c_sz = GLA_CHUNK
    ri = lax.broadcasted_iota(jnp.int32, (c_sz, c_sz), 0)
    ci = lax.broadcasted_iota(jnp.int32, (c_sz, c_sz), 1)
    tril = (ci <= ri).astype(BF16)
    gain = g_ref[...]
    wdec = wdec_ref[...].astype(BF16)
    for t in range(x_ref.shape[0] // NORM_SUB_ROWS):
        rows = slice(t * NORM_SUB_ROWS, (t + 1) * NORM_SUB_ROWS)
        h = _rms_rows(x_ref[rows, :], gain).astype(BF16)
        h_ref[rows, :] = h
        gdec = lax.dot_general(h, wdec, NT_DIMS, preferred_element_type=F32)
        logit = jnp.dot(gdec.astype(BF16), wup_ref[...], preferred_element_type=F32) + bdec_ref[...]
        la = jnp.maximum(_log_sigmoid(logit) / GLA_GATE_NORMALIZER, GLA_LOG_DECAY_MIN)
        hi = la.astype(BF16)
        r1 = la - hi.astype(F32)
        mid = r1.astype(BF16)
        lo = (r1 - mid.astype(F32)).astype(BF16)
        for c in range(NORM_SUB_ROWS // c_sz):
            r = slice(c * c_sz, (c + 1) * c_sz)
            b = (jnp.dot(tril, hi[r], preferred_element_type=F32)
                 + jnp.dot(tril, mid[r], preferred_element_type=F32)
                 + jnp.dot(tril, lo[r], preferred_element_type=F32))
            la_ref[t * NORM_SUB_ROWS + c * c_sz:t * NORM_SUB_ROWS + (c + 1) * c_sz, :] = b * LOG2E


def norm_decay(x, gain, wdec, wup, bdec, *, tm=512):
    t, d = x.shape
    tm = min(tm, t)
    return pl.pallas_call(
        _norm_decay_kernel,
        grid=(t // tm,),
        in_specs=[
            pl.BlockSpec((tm, d), lambda i: (i, 0)),
            pl.BlockSpec((1, d), lambda i: (0, 0)),
            pl.BlockSpec((DECAY_PAD, d), lambda i: (0, 0)),
            pl.BlockSpec((DECAY_PAD, GLA_QK_WIDTH), lambda i: (0, 0)),
            pl.BlockSpec((1, GLA_QK_WIDTH), lambda i: (0, 0)),
        ],
        out_specs=[
            pl.BlockSpec((tm, d), lambda i: (i, 0)),
            pl.BlockSpec((tm, GLA_QK_WIDTH), lambda i: (i, 0)),
        ],
        out_shape=[
            jax.ShapeDtypeStruct((t, d), BF16),
            jax.ShapeDtypeStruct((t, GLA_QK_WIDTH), F32),
        ],
        compiler_params=_params(("arbitrary",)),
        name="norm_decay",
    )(x, gain, wdec, wup, bdec)


def _final_norm_kernel(x_ref, g_ref, o_ref):
    o_ref[...] = _rms_rows(x_ref[...], g_ref[...])


def final_norm(x, gain, *, tm=512):
    t, d = x.shape
    tm = min(tm, t)
    return pl.pallas_call(
        _final_norm_kernel,
        grid=(t // tm,),
        in_specs=[pl.BlockSpec((tm, d), lambda i: (i, 0)),
                  pl.BlockSpec((1, d), lambda i: (0, 0))],
        out_specs=pl.BlockSpec((tm, d), lambda i: (i, 0)),
        out_shape=jax.ShapeDtypeStruct((t, d), F32),
        compiler_params=_params(("arbitrary",)),
        name="final_norm",
    )(x, gain)


def _in_proj_kernel(h_ref, wt_ref, z_ref):
    w = wt_ref[0].astype(BF16)
    z_ref[...] = lax.dot_general(h_ref[...], w, NT_DIMS,
                                 preferred_element_type=F32).astype(z_ref.dtype)


def _w_in_row(j, tn):
    u = GLA_DECAY_RANK
    sg_lo, sk_lo, ga_lo = Z_SG // tn, Z_SK // tn, Z_GA // tn
    col = j * (tn // u)
    units = jnp.where(j < Z_SQ // tn, col,
                      jnp.where(j < sg_lo, col + (W_IN_SQ - Z_SQ) // u,
                                jnp.where(j < sk_lo, col + (W_IN_SG - Z_SG) // u,
                                          jnp.where(j < ga_lo, col + (W_IN_SK - Z_SK) // u,
                                                    col + (W_IN_GA - Z_GA) // u))))
    return units * u


def _in_proj_cast_kernel(h_ref, wt_ref, *refs):
    n = (len(refs) - 1) // 2
    _in_proj_kernel(h_ref, wt_ref, refs[n])
    for src, dst in zip(refs[:n], refs[n + 1:]):
        dst[...] = src[...].astype(dst.dtype)


def _cast_rows_per_step(rows, n_steps):
    sub = 16
    r = sub
    while rows % r or rows // r > n_steps:
        r += sub
    return r


PACK_ROWS = (256, 512, 1024)


def in_proj(h, wt, layer, *, tm, tn, packed=False, cast=(), pack_layers=None):
    t, d = h.shape
    tm = min(tm, t)
    n_j = Z_WIDTH // tn
    n_steps = (t // tm) * n_j
    step = lambda i, j: i * n_j + j
    if packed:
        w_spec = pl.BlockSpec((1, tn, d), lambda i, j: (layer, j, 0))
    else:
        w_spec = pl.BlockSpec((pl.Element(1), pl.Element(tn), pl.Element(d)),
                              lambda i, j: (layer, _w_in_row(j, tn), 0))
    in_specs = [
        pl.BlockSpec((tm, d), lambda i, j: (i, 0), pipeline_mode=pl.Buffered(1)),
        w_spec]
    out_specs = [pl.BlockSpec((tm, tn), lambda i, j: (i, j))]
    out_shape = [jax.ShapeDtypeStruct((t, Z_WIDTH), BF16)]
    operands = [h, wt]
    cast_in, cast_out = [], []
    for w in cast:
        rows, cols = w.shape
        r = _cast_rows_per_step(rows, n_steps)
        last = rows // r - 1
        spec = pl.BlockSpec((r, cols), lambda i, j, last=last: (jnp.minimum(step(i, j), last), 0))
        cast_in.append(spec)
        cast_out.append(spec)
        out_shape.append(jax.ShapeDtypeStruct((rows, cols), BF16))
        operands.append(w)
    if pack_layers is not None:
        first, count = pack_layers
        pr = next(r for r in PACK_ROWS if count * (Z_WIDTH // r) <= n_steps)
        nb = Z_WIDTH // pr
        last = count * nb - 1

        def blk(i, j):
            s_ = jnp.minimum(step(i, j), last)
            return s_ // nb, s_ % nb

        cast_in.append(pl.BlockSpec(
            (pl.Element(1), pl.Element(pr), pl.Element(d)),
            lambda i, j: (first + blk(i, j)[0], _w_in_row(blk(i, j)[1], pr), 0)))
        cast_out.append(pl.BlockSpec((1, pr, d), lambda i, j: (blk(i, j)[0], blk(i, j)[1], 0)))
        out_shape.append(jax.ShapeDtypeStruct((count, Z_WIDTH, d), BF16))
        operands.append(wt)
    has_cast = bool(cast_in)
    return pl.pallas_call(
        _in_proj_cast_kernel if has_cast else _in_proj_kernel,
        grid=(t // tm, n_j),
        in_specs=in_specs + cast_in,
        out_specs=out_specs + cast_out if has_cast else out_specs[0],
        out_shape=out_shape if has_cast else out_shape[0],
        compiler_params=_params(("arbitrary", "arbitrary"), IN_PROJ_VMEM_LIMIT_BYTES),
        name="in_proj",
    )(*operands)


def _out_proj_kernel(m_ref, w_ref, x_ref, o_ref):
    o_ref[...] = x_ref[...] + jnp.dot(m_ref[...], w_ref[...], preferred_element_type=F32)


def out_proj(merged, w, x, layer, *, tm=1024, tn=1024):
    t, d = merged.shape
    tm = min(tm, t)
    n = w.shape[1]
    return pl.pallas_call(
        _out_proj_kernel,
        grid=(t // tm, n // tn),
        in_specs=[pl.BlockSpec((tm, d), lambda i, j: (i, 0)),
                  pl.BlockSpec((d, tn), lambda i, j: (layer, j)),
                  pl.BlockSpec((tm, tn), lambda i, j: (i, j))],
        out_specs=pl.BlockSpec((tm, tn), lambda i, j: (i, j)),
        out_shape=jax.ShapeDtypeStruct((t, n), F32),
        compiler_params=_params(("arbitrary", "arbitrary")),
        name="out_proj",
    )(merged, w, x)


def _merge_kernel(ua_ref, ub_ref, wa_ref, wb_ref, ga_ref, gb_ref, bg_ref, o_ref):
    ya = jnp.dot(ua_ref[...], wa_ref[...], preferred_element_type=F32)
    yb = jnp.dot(ub_ref[...], wb_ref[...], preferred_element_type=F32)
    bg = bg_ref[...]
    sa = jax.nn.sigmoid(ga_ref[...].astype(F32) + bg[0:1, :])
    sb = jax.nn.sigmoid(gb_ref[...].astype(F32) + bg[1:2, :])
    o_ref[...] = (sa * ya + sb * yb).astype(o_ref.dtype)


def merge(ua, ub, wa, wb, z, bgates, layer, *, tm=1024, tn=1024):
    t, ka = ua.shape
    tm = min(tm, t)
    kb = ub.shape[1]
    n = wa.shape[1]
    ga_blk = Z_GA // tn
    gb_blk = Z_GB // tn
    return pl.pallas_call(
        _merge_kernel,
        grid=(t // tm, n // tn),
        in_specs=[pl.BlockSpec((tm, ka), lambda i, j: (i, 0)),
                  pl.BlockSpec((tm, kb), lambda i, j: (i, 0)),
                  pl.BlockSpec((ka, tn), lambda i, j: (layer, j)),
                  pl.BlockSpec((kb, tn), lambda i, j: (layer, j)),
                  pl.BlockSpec((tm, tn), lambda i, j: (i, ga_blk + j)),
                  pl.BlockSpec((tm, tn), lambda i, j: (i, gb_blk + j)),
                  pl.BlockSpec((2, tn), lambda i, j: (0, j))],
        out_specs=pl.BlockSpec((tm, tn), lambda i, j: (i, j)),
        out_shape=jax.ShapeDtypeStruct((t, n), BF16),
        compiler_params=_params(("arbitrary", "arbitrary")),
        name="merge",
    )(ua, ub, wa, wb, z, z, bgates)


def _gla_kernel(q_ref, k_ref, v_ref, gg_ref, b_ref_in, gain_ref, o_ref, st_ref, *, rows):
    @pl.when(pl.program_id(1) == 0)
    def _():
        st_ref[...] = jnp.zeros_like(st_ref)

    c_sz = GLA_CHUNK
    scale = GLA_DK ** -0.5
    ri = lax.broadcasted_iota(jnp.int32, (c_sz, c_sz), 0)
    ci = lax.broadcasted_iota(jnp.int32, (c_sz, c_sz), 1)
    causal = ci <= ri
    gain = gain_ref[...]
    tn = (((0,), (0,)), ((), ()))

    def stage_a(c, h):
        kcols = slice(h * GLA_DK, (h + 1) * GLA_DK)
        vcols = slice(h * GLA_DV, (h + 1) * GLA_DV)
        r = slice(c * c_sz, (c + 1) * c_sz)
        qf = q_ref[r, kcols].astype(F32) * scale
        kf = k_ref[r, kcols].astype(F32)
        v = v_ref[r, vcols]
        b = b_ref_in[r, kcols]
        b_mid = b[c_sz // 2:c_sz // 2 + 1, :]
        b_last = b[c_sz - 1:c_sz, :]
        decay = jnp.exp2(b_last)
        q_dec = qf * jnp.exp2(b)
        k_grow = kf * jnp.exp2(-b)
        q_intra = (q_dec * jnp.exp2(-b_mid)).astype(BF16)
        k_intra = (k_grow * jnp.exp2(b_mid)).astype(BF16)
        q_inter = q_dec.astype(BF16)
        k_state = (k_grow * decay).astype(BF16)

        a = lax.dot_general(q_intra, k_intra, NT_DIMS, preferred_element_type=F32)
        a = jnp.where(causal, a, 0.0).astype(BF16)
        st = st_ref[h]
        o_inter = lax.dot_general(q_inter, st.astype(BF16), NT_DIMS, preferred_element_type=F32)
        st_ref[h] = st * decay + lax.dot_general(v, k_state, tn, preferred_element_type=F32)
        return a, v, o_inter

    def stage_b(c, h, a, v, o_inter):
        vcols = slice(h * GLA_DV, (h + 1) * GLA_DV)
        r = slice(c * c_sz, (c + 1) * c_sz)
        o = jnp.dot(a, v, preferred_element_type=F32) + o_inter
        y = _rms_rows(o, gain)
        g = gg_ref[r, vcols].astype(F32)
        o_ref[r, vcols] = (y * (g * jax.nn.sigmoid(g))).astype(o_ref.dtype)

    work = [(c, h) for c in range(rows // c_sz) for h in range(GLA_HEADS)]
    ahead = GLA_AHEAD
    pending = [stage_a(*work[i]) for i in range(ahead)]
    for i, (c, h) in enumerate(work):
        if i + ahead < len(work):
            pending.append(stage_a(*work[i + ahead]))
        stage_b(c, h, *pending.pop(0))


def gla(z, la, gain, *, batch, rows=512):
    t = z.shape[0]
    steps = t // batch // rows
    qw, vw = GLA_QK_WIDTH, GLA_V_WIDTH
    row_map = lambda b, i: b * steps + i
    return pl.pallas_call(
        functools.partial(_gla_kernel, rows=rows),
        grid=(batch, steps),
        in_specs=[
            pl.BlockSpec((rows, qw), lambda b, i: (row_map(b, i), Z_GQ // qw)),
            pl.BlockSpec((rows, qw), lambda b, i: (row_map(b, i), Z_GK // qw)),
            pl.BlockSpec((rows, vw), lambda b, i: (row_map(b, i), Z_GV // vw)),
            pl.BlockSpec((rows, vw), lambda b, i: (row_map(b, i), Z_GG // vw)),
            pl.BlockSpec((rows, qw), lambda b, i: (row_map(b, i), 0)),
            pl.BlockSpec((1, GLA_DV), lambda b, i: (0, 0)),
        ],
        out_specs=pl.BlockSpec((rows, vw), lambda b, i: (row_map(b, i), 0)),
        out_shape=jax.ShapeDtypeStruct((t, vw), BF16),
        scratch_shapes=[pltpu.VMEM((GLA_HEADS, GLA_DV, GLA_DK), F32)],
        compiler_params=_params(("arbitrary", "arbitrary")),
        name="gla",
    )(z, z, z, z, la, gain)


def _swa_kernel(sink_ref, q_ref, kp_ref, kc_ref, vp_ref, vc_ref, sg_ref, o_ref, *, blocks):
    w = SWA_WINDOW
    hd = SWA_HEAD_DIM
    rep = SWA_Q_HEADS // SWA_KV_HEADS
    scale = hd ** -0.5 * LOG2E
    first_prev_key = jnp.where(pl.program_id(1) > 0, 0, w)

    kj = lax.broadcasted_iota(jnp.int32, (w, w), 0)
    qi = lax.broadcasted_iota(jnp.int32, (w, w), 1)
    upper = kj > qi
    use_prev_first = upper & (kj >= first_prev_key)
    lane = lax.broadcasted_iota(jnp.int32, (w, LANES), 1)
    head_lanes = (lane < hd, lane >= hd)
    lane2 = lax.broadcasted_iota(jnp.int32, (2 * w, LANES), 1)
    half_lanes = (lane2 < hd, lane2 >= hd)
    tn_dims = (((0,), (0,)), ((), ()))

    def band(prev_ref, cur_ref, blk, cols):
        cur = cur_ref[blk * w:(blk + 1) * w, cols]
        prev = prev_ref[:, cols] if blk == 0 else cur_ref[(blk - 1) * w:blk * w, cols]
        return jnp.concatenate([prev, cur], axis=0)

    def scores(blk, g):
        pair, sub = divmod(g, 2)
        cols = slice(pair * LANES, (pair + 1) * LANES)
        k2 = band(kp_ref, kc_ref, blk, cols) * scale
        kg = jnp.where(half_lanes[sub], k2, pltpu.roll(k2, hd, axis=1))
        qs = []
        for hp in range(rep // 2):
            head0 = g * rep + 2 * hp
            q2 = q_ref[blk * w:(blk + 1) * w, head0 * hd:(head0 + 2) * hd]
            for e in range(2):
                qs.append(jnp.where(head_lanes[e], q2, jnp.zeros_like(q2)))
        return lax.dot_general(kg, jnp.concatenate(qs, axis=0), NT_DIMS,
                               preferred_element_type=F32)

    work = [(blk, g) for blk in range(blocks) for g in range(SWA_KV_HEADS)]
    pending = [scores(*work[i]) for i in range(SWA_AHEAD)]
    for n, (blk, g) in enumerate(work):
        st = pending.pop(0)
        if n + SWA_AHEAD < len(work):
            pending.append(scores(*work[n + SWA_AHEAD]))
        pair, sub = divmod(g, 2)
        cols = slice(pair * LANES, (pair + 1) * LANES)
        qrows = slice(blk * w, (blk + 1) * w)
        v2 = band(vp_ref, vc_ref, blk, cols)
        use_prev = use_prev_first if blk == 0 else upper
        ps, invs = [], []
        for i in range(rep):
            sblk = st[:, i * w:(i + 1) * w]
            s = jnp.where(use_prev, sblk[:w], jnp.where(upper, NEG_BIG, sblk[w:]))
            sink = sink_ref[g * rep + i] * LOG2E
            m = jnp.maximum(jnp.max(s, axis=0, keepdims=True), sink)
            p = jnp.exp2(s - m)
            denom = jnp.sum(p, axis=0, keepdims=True) + jnp.exp2(sink - m)
            invs.append(1.0 / denom)
            ps.append(jnp.concatenate([jnp.where(upper, p, 0.0), jnp.where(upper, 0.0, p)],
                                      axis=0).astype(BF16))
        ot = lax.dot_general(v2, jnp.concatenate(ps, axis=1), tn_dims,
                             preferred_element_type=F32)
        drows = slice(sub * hd, (sub + 1) * hd)
        for hp in range(rep // 2):
            head0 = g * rep + 2 * hp
            qcols = slice(head0 * hd, (head0 + 2) * hd)
            o_a = ot[drows, (2 * hp) * w:(2 * hp + 1) * w] * invs[2 * hp]
            o_b = ot[drows, (2 * hp + 1) * w:(2 * hp + 2) * w] * invs[2 * hp + 1]
            o = jnp.concatenate([o_a, o_b], axis=0).T
            gate = sg_ref[qrows, qcols].astype(F32)
            o_ref[qrows, qcols] = (o * (gate * jax.nn.sigmoid(gate))).astype(o_ref.dtype)


def swa(z, sinks, *, batch, blocks=4):
    t = z.shape[0]
    w = SWA_WINDOW
    rows = blocks * w
    steps = t // batch // rows
    qw, kw = SWA_Q_WIDTH, SWA_KV_WIDTH
    cur = lambda b, i: b * steps + i
    prev = lambda b, i: (b * steps + i) * blocks - jnp.where(i > 0, 1, 0)
    return pl.pallas_call(
        functools.partial(_swa_kernel, blocks=blocks),
        grid=(batch, steps),
        in_specs=[
            pl.BlockSpec(memory_space=pltpu.SMEM),
            pl.BlockSpec((rows, qw), lambda b, i: (cur(b, i), Z_SQ // qw)),
            pl.BlockSpec((w, kw), lambda b, i: (prev(b, i), Z_SK // kw)),
            pl.BlockSpec((rows, kw), lambda b, i: (cur(b, i), Z_SK // kw)),
            pl.BlockSpec((w, kw), lambda b, i: (prev(b, i), Z_SV // kw)),
            pl.BlockSpec((rows, kw), lambda b, i: (cur(b, i), Z_SV // kw)),
            pl.BlockSpec((rows, qw), lambda b, i: (cur(b, i), Z_SG // qw)),
        ],
        out_specs=pl.BlockSpec((rows, qw), lambda b, i: (cur(b, i), 0)),
        out_shape=jax.ShapeDtypeStruct((t, qw), BF16),
        compiler_params=_params(("arbitrary", "arbitrary")),
        name="swa",
    )(sinks, z, z, z, z, z, z)


def kernel(x, norm_gains, w_in, b_gates, w_decay_up, b_decay, gla_norm_gains, sinks,
           w_gla_out, w_swa_out, w_out, final_norm_gain):
    batch, seq, d = x.shape
    depth = w_in.shape[0]
    xt = x.reshape(batch * seq, d)
    w_in_t = jnp.swapaxes(w_in, 1, 2)
    to_cast = (w_gla_out.reshape(depth * GLA_V_WIDTH, d), w_swa_out.reshape(depth * SWA_Q_WIDTH, d),
               w_out.reshape(depth * d, d))
    pad_dec = DECAY_PAD - GLA_DECAY_RANK
    for l in range(depth):
        w_dec = jnp.pad(w_in_t[l, W_IN_DECAY:W_IN_SQ, :], ((0, pad_dec), (0, 0)))
        w_up = jnp.pad(w_decay_up[l], ((0, pad_dec), (0, 0))).astype(BF16)
        h, la = norm_decay(xt, norm_gains[l][None, :], w_dec, w_up, b_decay[l][None, :])
        if l == 0:
            z, w_ga, w_sw, w_o, w_in_rest = in_proj(
                h, w_in_t, l, tm=2048, tn=512, cast=to_cast,
                pack_layers=(1, depth - 1) if depth > 1 else None) + ((None,) if depth == 1 else ())
        else:
            z = in_proj(h, w_in_rest, l - 1, tm=2048, tn=1024, packed=True)
        ua = gla(z, la, gla_norm_gains[l][None, :], batch=batch)
        ub = swa(z, sinks[l], batch=batch)
        merged = merge(ua, ub, w_ga, w_sw, z, b_gates[l], l)
        xt = out_proj(merged, w_o, xt, l)
    out = final_norm(xt, final_norm_gain[None, :])
    return out.reshape(batch, seq, d)
```

```python
import functools

import jax
import jax.numpy as jnp
from jax import lax
from jax.experimental import pallas as pl
from jax.experimental.pallas import tpu as pltpu

F32 = jnp.float32
BF16 = jnp.bfloat16

D_MODEL = 4096
GLA_HEADS = 8
GLA_DK = 128
GLA_DV = 256
GLA_QK_WIDTH = GLA_HEADS * GLA_DK
GLA_V_WIDTH = GLA_HEADS * GLA_DV
GLA_DECAY_RANK = 16
GLA_GATE_NORMALIZER = 16.0
GLA_LOG_DECAY_MIN = -1.0
GLA_CHUNK = 64
GLA_AHEAD = 2
SWA_Q_HEADS = 32
SWA_KV_HEADS = 8
SWA_HEAD_DIM = 64
SWA_WINDOW = 128
SWA_AHEAD = 2
SWA_Q_WIDTH = SWA_Q_HEADS * SWA_HEAD_DIM
SWA_KV_WIDTH = SWA_KV_HEADS * SWA_HEAD_DIM
RMS_EPS = 1e-6

LANES = 128
DECAY_PAD = LANES

W_IN_DECAY = 2 * GLA_QK_WIDTH + 2 * GLA_V_WIDTH
W_IN_SQ = W_IN_DECAY + GLA_DECAY_RANK
W_IN_SK = W_IN_SQ + SWA_Q_WIDTH
W_IN_SG = W_IN_SK + 2 * SWA_KV_WIDTH
W_IN_GA = W_IN_SG + SWA_Q_WIDTH
Z_GQ = 0
Z_GK = Z_GQ + GLA_QK_WIDTH
Z_GV = Z_GK + GLA_QK_WIDTH
Z_GG = Z_GV + GLA_V_WIDTH
Z_SQ = Z_GG + GLA_V_WIDTH
Z_SG = Z_SQ + SWA_Q_WIDTH
Z_SK = Z_SG + SWA_Q_WIDTH
Z_SV = Z_SK + SWA_KV_WIDTH
Z_GA = Z_SV + SWA_KV_WIDTH
Z_GB = Z_GA + D_MODEL
Z_WIDTH = Z_GB + D_MODEL

VMEM_LIMIT_BYTES = 56 * 1024 * 1024
IN_PROJ_VMEM_LIMIT_BYTES = 61 * 1024 * 1024

NEG_BIG = -1e30
LOG2E = 1.4426950408889634
NT_DIMS = (((1,), (1,)), ((), ()))


def _params(semantics, vmem_limit_bytes=VMEM_LIMIT_BYTES):
    return pltpu.CompilerParams(dimension_semantics=semantics,
                                vmem_limit_bytes=vmem_limit_bytes)


def _rms_rows(x, gain):
    ms = jnp.mean(x * x, axis=-1, keepdims=True)
    return x * lax.rsqrt(ms + RMS_EPS) * gain


def _log_sigmoid(x):
    return jnp.minimum(x, 0.0) - jnp.log(1.0 + jnp.exp(-jnp.abs(x)))


NORM_SUB_ROWS = 128


def _chunk_log_decay(gdec, wup, bdec, tril, b_ref, row0):
    c_sz = GLA_CHUNK
    logit = jnp.dot(gdec.astype(BF16), wup, preferred_element_type=F32) + bdec
    la = jnp.maximum(_log_sigmoid(logit) / GLA_GATE_NORMALIZER, GLA_LOG_DECAY_MIN)
    hi = la.astype(BF16)
    r1 = la - hi.astype(F32)
    mid = r1.astype(BF16)
    lo = (r1 - mid.astype(F32)).astype(BF16)
    for c in range(gdec.shape[0] // c_sz):
        r = slice(c * c_sz, (c + 1) * c_sz)
        b = (jnp.dot(tril, hi[r], preferred_element_type=F32)
             + jnp.dot(tril, mid[r], preferred_element_type=F32)
             + jnp.dot(tril, lo[r], preferred_element_type=F32))
        b_ref[row0 + c * c_sz:row0 + (c + 1) * c_sz, :] = b * LOG2E


def _tril_chunk():
    c_sz = GLA_CHUNK
    ri = lax.broadcasted_iota(jnp.int32, (c_sz, c_sz), 0)
    ci = lax.broadcasted_iota(jnp.int32, (c_sz, c_sz), 1)
    return (ci <= ri).astype(BF16)


def _norm_decay_kernel(x_ref, g_ref, wdec_ref, wup_ref, bdec_ref, h_ref, la_ref):
    tril = _tril_chunk()
    gain = g_ref[...]
    wdec = wdec_ref[...].astype(BF16)
    for t in range(x_ref.shape[0] // NORM_SUB_ROWS):
        rows = slice(t * NORM_SUB_ROWS, (t + 1) * NORM_SUB_ROWS)
        h = _rms_rows(x_ref[rows, :], gain).astype(BF16)
        h_ref[rows, :] = h
        gdec = lax.dot_general(h, wdec, NT_DIMS, preferred_element_type=F32)
        _chunk_log_decay(gdec, wup_ref[...], bdec_ref[...], tril, la_ref, t * NORM_SUB_ROWS)


def _decay_kernel(gdec_ref, ssq_ref, wup_ref, bdec_ref, rinv_ref, b_ref, *, d):
    tril = _tril_chunk()
    for t in range(gdec_ref.shape[0] // NORM_SUB_ROWS):
        rows = slice(t * NORM_SUB_ROWS, (t + 1) * NORM_SUB_ROWS)
        rinv = lax.rsqrt(ssq_ref[rows, :] * (1.0 / d) + RMS_EPS)
        rinv_ref[rows, :] = rinv
        _chunk_log_decay(gdec_ref[rows, :] * rinv, wup_ref[...], bdec_ref[...], tril, b_ref,
                         t * NORM_SUB_ROWS)


def decay(gdec, ssq, wup, bdec, *, d, tm=1024):
    t = gdec.shape[0]
    tm = min(tm, t)
    return pl.pallas_call(
        functools.partial(_decay_kernel, d=d),
        grid=(t // tm,),
        in_specs=[pl.BlockSpec((tm, DECAY_PAD), lambda i: (i, 0)),
                  pl.BlockSpec((tm, LANES), lambda i: (i, 0)),
                  pl.BlockSpec((DECAY_PAD, GLA_QK_WIDTH), lambda i: (0, 0)),
                  pl.BlockSpec((1, GLA_QK_WIDTH), lambda i: (0, 0))],
        out_specs=[pl.BlockSpec((tm, LANES), lambda i: (i, 0)),
                   pl.BlockSpec((tm, GLA_QK_WIDTH), lambda i: (i, 0))],
        out_shape=[jax.ShapeDtypeStruct((t, LANES), F32),
                   jax.ShapeDtypeStruct((t, GLA_QK_WIDTH), F32)],
        compiler_params=_params(("arbitrary",)),
        name="decay",
    )(gdec, ssq, wup, bdec)


def norm_decay(x, gain, wdec, wup, bdec, *, tm=512):
    t, d = x.shape
    tm = min(tm, t)
    return pl.pallas_call(
        _norm_decay_kernel,
        grid=(t // tm,),
        in_specs=[
            pl.BlockSpec((tm, d), lambda i: (i, 0)),
            pl.BlockSpec((1, d), lambda i: (0, 0)),
            pl.BlockSpec((DECAY_PAD, d), lambda i: (0, 0)),
            pl.BlockSpec((DECAY_PAD, GLA_QK_WIDTH), lambda i: (0, 0)),
            pl.BlockSpec((1, GLA_QK_WIDTH), lambda i: (0, 0)),
        ],
        out_specs=[
            pl.BlockSpec((tm, d), lambda i: (i, 0)),
            pl.BlockSpec((tm, GLA_QK_WIDTH), lambda i: (i, 0)),
        ],
        out_shape=[
            jax.ShapeDtypeStruct((t, d), BF16),
            jax.ShapeDtypeStruct((t, GLA_QK_WIDTH), F32),
        ],
        compiler_params=_params(("arbitrary",)),
        name="norm_decay",
    )(x, gain, wdec, wup, bdec)


def _final_norm_kernel(x_ref, g_ref, o_ref):
    o_ref[...] = _rms_rows(x_ref[...], g_ref[...])


def final_norm(x, gain, *, tm=512):
    t, d = x.shape
    tm = min(tm, t)
    return pl.pallas_call(
        _final_norm_kernel,
        grid=(t // tm,),
        in_specs=[pl.BlockSpec((tm, d), lambda i: (i, 0)),
                  pl.BlockSpec((1, d), lambda i: (0, 0))],
        out_specs=pl.BlockSpec((tm, d), lambda i: (i, 0)),
        out_shape=jax.ShapeDtypeStruct((t, d), F32),
        compiler_params=_params(("arbitrary",)),
        name="final_norm",
    )(x, gain)


def _in_proj_kernel(h_ref, wt_ref, z_ref, rinv_ref=None):
    w = wt_ref[0].astype(BF16)
    acc = lax.dot_general(h_ref[...], w, NT_DIMS, preferred_element_type=F32)
    if rinv_ref is not None:
        acc = acc * rinv_ref[:, 0:1]
    z_ref[...] = acc.astype(z_ref.dtype)


def _in_proj_scaled_kernel(h_ref, wt_ref, rinv_ref, z_ref):
    _in_proj_kernel(h_ref, wt_ref, z_ref, rinv_ref)


def _w_in_row(j, tn):
    u = GLA_DECAY_RANK
    sg_lo, sk_lo, ga_lo = Z_SG // tn, Z_SK // tn, Z_GA // tn
    col = j * (tn // u)
    units = jnp.where(j < Z_SQ // tn, col,
                      jnp.where(j < sg_lo, col + (W_IN_SQ - Z_SQ) // u,
                                jnp.where(j < sk_lo, col + (W_IN_SG - Z_SG) // u,
                                          jnp.where(j < ga_lo, col + (W_IN_SK - Z_SK) // u,
                                                    col + (W_IN_GA - Z_GA) // u))))
    return units * u


def _in_proj_cast_kernel(h_ref, wt_ref, *refs, n_cast, pack):
    n_in = n_cast + (2 if pack else 0)
    _in_proj_kernel(h_ref, wt_ref, refs[n_in])
    for src, dst in zip(refs[:n_cast], refs[n_in + 1:]):
        dst[...] = src[...].astype(dst.dtype)
    if pack:
        refs[-1][...] = (refs[n_cast][...] * refs[n_cast + 1][...]).astype(refs[-1].dtype)


def _cast_rows_per_step(rows, n_steps):
    sub = 16
    r = sub
    while rows % r or rows // r > n_steps:
        r += sub
    return r


PACK_ROWS = (256, 512, 1024)


def in_proj(h, wt, layer, *, tm, tn, packed=False, rinv=None, cast=(), pack_layers=None,
            pack_gains=None):
    t, d = h.shape
    tm = min(tm, t)
    n_j = Z_WIDTH // tn
    n_steps = (t // tm) * n_j
    step = lambda i, j: i * n_j + j
    if packed:
        w_spec = pl.BlockSpec((1, tn, d), lambda i, j: (layer, j, 0))
    else:
        w_spec = pl.BlockSpec((pl.Element(1), pl.Element(tn), pl.Element(d)),
                              lambda i, j: (layer, _w_in_row(j, tn), 0))
    in_specs = [
        pl.BlockSpec((tm, d), lambda i, j: (i, 0), pipeline_mode=pl.Buffered(1)),
        w_spec]
    out_specs = [pl.BlockSpec((tm, tn), lambda i, j: (i, j))]
    out_shape = [jax.ShapeDtypeStruct((t, Z_WIDTH), BF16)]
    operands = [h, wt]
    if rinv is not None:
        assert not cast and pack_layers is None
        in_specs.append(pl.BlockSpec((tm, LANES), lambda i, j: (i, 0), pipeline_mode=pl.Buffered(1)))
        operands.append(rinv)
        body = _in_proj_scaled_kernel
    else:
        body = _in_proj_kernel
    cast_in, cast_out = [], []
    for w in cast:
        rows, cols = w.shape
        r = _cast_rows_per_step(rows, n_steps)
        last = rows // r - 1
        spec = pl.BlockSpec((r, cols), lambda i, j, last=last: (jnp.minimum(step(i, j), last), 0))
        cast_in.append(spec)
        cast_out.append(spec)
        out_shape.append(jax.ShapeDtypeStruct((rows, cols), BF16))
        operands.append(w)
    if pack_layers is not None:
        first, count = pack_layers
        pr = next(r for r in PACK_ROWS if count * (Z_WIDTH // r) <= n_steps)
        nb = Z_WIDTH // pr
        last = count * nb - 1

        def blk(i, j):
            s_ = jnp.minimum(step(i, j), last)
            return s_ // nb, s_ % nb

        cast_in.append(pl.BlockSpec(
            (pl.Element(1), pl.Element(pr), pl.Element(d)),
            lambda i, j: (first + blk(i, j)[0], _w_in_row(blk(i, j)[1], pr), 0)))
        cast_in.append(pl.BlockSpec((1, 1, d), lambda i, j: (first + blk(i, j)[0], 0, 0)))
        cast_out.append(pl.BlockSpec((1, pr, d), lambda i, j: (blk(i, j)[0], blk(i, j)[1], 0)))
        out_shape.append(jax.ShapeDtypeStruct((count, Z_WIDTH, d), BF16))
        operands += [wt, pack_gains]
    has_cast = bool(cast_in)
    if has_cast:
        body = functools.partial(_in_proj_cast_kernel, n_cast=len(cast), pack=pack_layers is not None)
    return pl.pallas_call(
        body,
        grid=(t // tm, n_j),
        in_specs=in_specs + cast_in,
        out_specs=out_specs + cast_out if has_cast else out_specs[0],
        out_shape=out_shape if has_cast else out_shape[0],
        compiler_params=_params(("arbitrary", "arbitrary"), IN_PROJ_VMEM_LIMIT_BYTES),
        name="in_proj",
    )(*operands)


def _out_proj_kernel(m_ref, w_ref, x_ref, o_ref):
    o_ref[...] = x_ref[...] + jnp.dot(m_ref[...], w_ref[...], preferred_element_type=F32)


def out_proj(merged, w, x, layer, *, tm=1024, tn=1024):
    t, d = merged.shape
    tm = min(tm, t)
    n = w.shape[1]
    return pl.pallas_call(
        _out_proj_kernel,
        grid=(t // tm, n // tn),
        in_specs=[pl.BlockSpec((tm, d), lambda i, j: (i, 0)),
                  pl.BlockSpec((d, tn), lambda i, j: (layer, j)),
                  pl.BlockSpec((tm, tn), lambda i, j: (i, j))],
        out_specs=pl.BlockSpec((tm, tn), lambda i, j: (i, j)),
        out_shape=jax.ShapeDtypeStruct((t, n), F32),
        compiler_params=_params(("arbitrary", "arbitrary")),
        name="out_proj",
    )(merged, w, x)


def _out_proj_stats_kernel(m_ref, w_ref, x_ref, wdec_ref, g_ref, o_ref, xb_ref, ssq_ref, gdec_ref):
    xn = x_ref[...] + jnp.dot(m_ref[...], w_ref[...], preferred_element_type=F32)
    o_ref[...] = xn
    xb = xn.astype(BF16)
    xb_ref[...] = xb
    ssq = jnp.broadcast_to(jnp.sum(xn * xn, axis=-1, keepdims=True), ssq_ref.shape)
    wdec = (wdec_ref[...] * g_ref[...]).astype(BF16)
    gdec = lax.dot_general(xb, wdec, NT_DIMS, preferred_element_type=F32)

    @pl.when(pl.program_id(1) == 0)
    def _():
        ssq_ref[...] = ssq
        gdec_ref[...] = gdec

    @pl.when(pl.program_id(1) > 0)
    def _():
        ssq_ref[...] += ssq
        gdec_ref[...] += gdec


def out_proj_stats(merged, w, x, layer, wdec_next, gain_next, *, tm=1024, tn=1024):
    t, d = merged.shape
    tm = min(tm, t)
    n = w.shape[1]
    return pl.pallas_call(
        _out_proj_stats_kernel,
        grid=(t // tm, n // tn),
        in_specs=[pl.BlockSpec((tm, d), lambda i, j: (i, 0), pipeline_mode=pl.Buffered(1)),
                  pl.BlockSpec((d, tn), lambda i, j: (layer, j)),
                  pl.BlockSpec((tm, tn), lambda i, j: (i, j)),
                  pl.BlockSpec((DECAY_PAD, tn), lambda i, j: (0, j)),
                  pl.BlockSpec((1, tn), lambda i, j: (0, j))],
        out_specs=[pl.BlockSpec((tm, tn), lambda i, j: (i, j)),
                   pl.BlockSpec((tm, tn), lambda i, j: (i, j)),
                   pl.BlockSpec((tm, LANES), lambda i, j: (i, 0)),
                   pl.BlockSpec((tm, DECAY_PAD), lambda i, j: (i, 0))],
        out_shape=[jax.ShapeDtypeStruct((t, n), F32),
                   jax.ShapeDtypeStruct((t, n), BF16),
                   jax.ShapeDtypeStruct((t, LANES), F32),
                   jax.ShapeDtypeStruct((t, DECAY_PAD), F32)],
        compiler_params=_params(("arbitrary", "arbitrary"), IN_PROJ_VMEM_LIMIT_BYTES),
        name="out_proj_stats",
    )(merged, w, x, wdec_next, gain_next)


def _merge_kernel(ua_ref, ub_ref, wa_ref, wb_ref, ga_ref, gb_ref, bg_ref, o_ref):
    ya = jnp.dot(ua_ref[...], wa_ref[...], preferred_element_type=F32)
    yb = jnp.dot(ub_ref[...], wb_ref[...], preferred_element_type=F32)
    bg = bg_ref[...]
    sa = jax.nn.sigmoid(ga_ref[...].astype(F32) + bg[0:1, :])
    sb = jax.nn.sigmoid(gb_ref[...].astype(F32) + bg[1:2, :])
    o_ref[...] = (sa * ya + sb * yb).astype(o_ref.dtype)


def merge(ua, ub, wa, wb, z, bgates, layer, *, tm=1024, tn=1024):
    t, ka = ua.shape
    tm = min(tm, t)
    kb = ub.shape[1]
    n = wa.shape[1]
    ga_blk = Z_GA // tn
    gb_blk = Z_GB // tn
    return pl.pallas_call(
        _merge_kernel,
        grid=(t // tm, n // tn),
        in_specs=[pl.BlockSpec((tm, ka), lambda i, j: (i, 0)),
                  pl.BlockSpec((tm, kb), lambda i, j: (i, 0)),
                  pl.BlockSpec((ka, tn), lambda i, j: (layer, j)),
                  pl.BlockSpec((kb, tn), lambda i, j: (layer, j)),
                  pl.BlockSpec((tm, tn), lambda i, j: (i, ga_blk + j)),
                  pl.BlockSpec((tm, tn), lambda i, j: (i, gb_blk + j)),
                  pl.BlockSpec((2, tn), lambda i, j: (0, j))],
        out_specs=pl.BlockSpec((tm, tn), lambda i, j: (i, j)),
        out_shape=jax.ShapeDtypeStruct((t, n), BF16),
        compiler_params=_params(("arbitrary", "arbitrary")),
        name="merge",
    )(ua, ub, wa, wb, z, z, bgates)


def _gla_kernel(q_ref, k_ref, v_ref, gg_ref, b_ref_in, gain_ref, o_ref, st_ref, *, rows):
    @pl.when(pl.program_id(1) == 0)
    def _():
        st_ref[...] = jnp.zeros_like(st_ref)

    c_sz = GLA_CHUNK
    scale = GLA_DK ** -0.5
    ri = lax.broadcasted_iota(jnp.int32, (c_sz, c_sz), 0)
    ci = lax.broadcasted_iota(jnp.int32, (c_sz, c_sz), 1)
    causal = ci <= ri
    gain = gain_ref[...]
    tn = (((0,), (0,)), ((), ()))

    def stage_a(c, h):
        kcols = slice(h * GLA_DK, (h + 1) * GLA_DK)
        vcols = slice(h * GLA_DV, (h + 1) * GLA_DV)
        r = slice(c * c_sz, (c + 1) * c_sz)
        qf = q_ref[r, kcols].astype(F32) * scale
        kf = k_ref[r, kcols].astype(F32)
        v = v_ref[r, vcols]
        b = b_ref_in[r, kcols]
        b_mid = b[c_sz // 2:c_sz // 2 + 1, :]
        b_last = b[c_sz - 1:c_sz, :]
        decay = jnp.exp2(b_last)
        q_dec = qf * jnp.exp2(b)
        k_grow = kf * jnp.exp2(-b)
        q_intra = (q_dec * jnp.exp2(-b_mid)).astype(BF16)
        k_intra = (k_grow * jnp.exp2(b_mid)).astype(BF16)
        q_inter = q_dec.astype(BF16)
        k_state = (k_grow * decay).astype(BF16)

        a = lax.dot_general(q_intra, k_intra, NT_DIMS, preferred_element_type=F32)
        a = jnp.where(causal, a, 0.0).astype(BF16)
        st = st_ref[h]
        o_inter = lax.dot_general(q_inter, st.astype(BF16), NT_DIMS, preferred_element_type=F32)
        st_ref[h] = st * decay + lax.dot_general(v, k_state, tn, preferred_element_type=F32)
        return a, v, o_inter

    def stage_b(c, h, a, v, o_inter):
        vcols = slice(h * GLA_DV, (h + 1) * GLA_DV)
        r = slice(c * c_sz, (c + 1) * c_sz)
        o = jnp.dot(a, v, preferred_element_type=F32) + o_inter
        y = _rms_rows(o, gain)
        g = gg_ref[r, vcols].astype(F32)
        o_ref[r, vcols] = (y * (g * jax.nn.sigmoid(g))).astype(o_ref.dtype)

    work = [(c, h) for c in range(rows // c_sz) for h in range(GLA_HEADS)]
    ahead = GLA_AHEAD
    pending = [stage_a(*work[i]) for i in range(ahead)]
    for i, (c, h) in enumerate(work):
        if i + ahead < len(work):
            pending.append(stage_a(*work[i + ahead]))
        stage_b(c, h, *pending.pop(0))


def gla(z, la, gain, *, batch, rows=1024):
    t = z.shape[0]
    rows = min(rows, t // batch)
    steps = t // batch // rows
    qw, vw = GLA_QK_WIDTH, GLA_V_WIDTH
    row_map = lambda b, i: b * steps + i
    return pl.pallas_call(
        functools.partial(_gla_kernel, rows=rows),
        grid=(batch, steps),
        in_specs=[
            pl.BlockSpec((rows, qw), lambda b, i: (row_map(b, i), Z_GQ // qw)),
            pl.BlockSpec((rows, qw), lambda b, i: (row_map(b, i), Z_GK // qw)),
            pl.BlockSpec((rows, vw), lambda b, i: (row_map(b, i), Z_GV // vw)),
            pl.BlockSpec((rows, vw), lambda b, i: (row_map(b, i), Z_GG // vw)),
            pl.BlockSpec((rows, qw), lambda b, i: (row_map(b, i), 0)),
            pl.BlockSpec((1, GLA_DV), lambda b, i: (0, 0)),
        ],
        out_specs=pl.BlockSpec((rows, vw), lambda b, i: (row_map(b, i), 0)),
        out_shape=jax.ShapeDtypeStruct((t, vw), BF16),
        scratch_shapes=[pltpu.VMEM((GLA_HEADS, GLA_DV, GLA_DK), F32)],
        compiler_params=_params(("arbitrary", "arbitrary")),
        name="gla",
    )(z, z, z, z, la, gain)


def _swa_kernel(sink_ref, q_ref, kp_ref, kc_ref, vp_ref, vc_ref, sg_ref, o_ref, *, blocks):
    w = SWA_WINDOW
    hd = SWA_HEAD_DIM
    rep = SWA_Q_HEADS // SWA_KV_HEADS
    scale = hd ** -0.5 * LOG2E
    first_prev_key = jnp.where(pl.program_id(1) > 0, 0, w)

    kj = lax.broadcasted_iota(jnp.int32, (w, w), 0)
    qi = lax.broadcasted_iota(jnp.int32, (w, w), 1)
    upper = kj > qi
    use_prev_first = upper & (kj >= first_prev_key)
    lane = lax.broadcasted_iota(jnp.int32, (w, LANES), 1)
    head_lanes = (lane < hd, lane >= hd)
    lane2 = lax.broadcasted_iota(jnp.int32, (2 * w, LANES), 1)
    half_lanes = (lane2 < hd, lane2 >= hd)
    tn_dims = (((0,), (0,)), ((), ()))

    def band(prev_ref, cur_ref, blk, cols):
        cur = cur_ref[blk * w:(blk + 1) * w, cols]
        prev = prev_ref[:, cols] if blk == 0 else cur_ref[(blk - 1) * w:blk * w, cols]
        return jnp.concatenate([prev, cur], axis=0)

    def scores(blk, g):
        pair, sub = divmod(g, 2)
        cols = slice(pair * LANES, (pair + 1) * LANES)
        k2 = band(kp_ref, kc_ref, blk, cols) * scale
        kg = jnp.where(half_lanes[sub], k2, pltpu.roll(k2, hd, axis=1))
        qs = []
        for hp in range(rep // 2):
            head0 = g * rep + 2 * hp
            q2 = q_ref[blk * w:(blk + 1) * w, head0 * hd:(head0 + 2) * hd]
            for e in range(2):
                qs.append(jnp.where(head_lanes[e], q2, jnp.zeros_like(q2)))
        return lax.dot_general(kg, jnp.concatenate(qs, axis=0), NT_DIMS,
                               preferred_element_type=F32)

    work = [(blk, g) for blk in range(blocks) for g in range(SWA_KV_HEADS)]
    pending = [scores(*work[i]) for i in range(SWA_AHEAD)]
    for n, (blk, g) in enumerate(work):
        st = pending.pop(0)
        if n + SWA_AHEAD < len(work):
            pending.append(scores(*work[n + SWA_AHEAD]))
        pair, sub = divmod(g, 2)
        cols = slice(pair * LANES, (pair + 1) * LANES)
        qrows = slice(blk * w, (blk + 1) * w)
        v2 = band(vp_ref, vc_ref, blk, cols)
        use_prev = use_prev_first if blk == 0 else upper
        ps, invs = [], []
        for i in range(rep):
            sblk = st[:, i * w:(i + 1) * w]
            s = jnp.where(use_prev, sblk[:w], jnp.where(upper, NEG_BIG, sblk[w:]))
            sink = sink_ref[g * rep + i] * LOG2E
            m = jnp.maximum(jnp.max(s, axis=0, keepdims=True), sink)
            p = jnp.exp2(s - m)
            denom = jnp.sum(p, axis=0, keepdims=True) + jnp.exp2(sink - m)
            invs.append(1.0 / denom)
            ps.append(jnp.concatenate([jnp.where(upper, p, 0.0), jnp.where(upper, 0.0, p)],
                                      axis=0).astype(BF16))
        ot = lax.dot_general(v2, jnp.concatenate(ps, axis=1), tn_dims,
                             preferred_element_type=F32)
        drows = slice(sub * hd, (sub + 1) * hd)
        for hp in range(rep // 2):
            head0 = g * rep + 2 * hp
            qcols = slice(head0 * hd, (head0 + 2) * hd)
            o_a = ot[drows, (2 * hp) * w:(2 * hp + 1) * w] * invs[2 * hp]
            o_b = ot[drows, (2 * hp + 1) * w:(2 * hp + 2) * w] * invs[2 * hp + 1]
            o = jnp.concatenate([o_a, o_b], axis=0).T
            gate = sg_ref[qrows, qcols].astype(F32)
            o_ref[qrows, qcols] = (o * (gate * jax.nn.sigmoid(gate))).astype(o_ref.dtype)


def swa(z, sinks, *, batch, blocks=8):
    t = z.shape[0]
    w = SWA_WINDOW
    blocks = min(blocks, t // batch // w)
    rows = blocks * w
    steps = t // batch // rows
    qw, kw = SWA_Q_WIDTH, SWA_KV_WIDTH
    cur = lambda b, i: b * steps + i
    prev = lambda b, i: (b * steps + i) * blocks - jnp.where(i > 0, 1, 0)
    return pl.pallas_call(
        functools.partial(_swa_kernel, blocks=blocks),
        grid=(batch, steps),
        in_specs=[
            pl.BlockSpec(memory_space=pltpu.SMEM),
            pl.BlockSpec((rows, qw), lambda b, i: (cur(b, i), Z_SQ // qw)),
            pl.BlockSpec((w, kw), lambda b, i: (prev(b, i), Z_SK // kw)),
            pl.BlockSpec((rows, kw), lambda b, i: (cur(b, i), Z_SK // kw)),
            pl.BlockSpec((w, kw), lambda b, i: (prev(b, i), Z_SV // kw)),
            pl.BlockSpec((rows, kw), lambda b, i: (cur(b, i), Z_SV // kw)),
            pl.BlockSpec((rows, qw), lambda b, i: (cur(b, i), Z_SG // qw)),
        ],
        out_specs=pl.BlockSpec((rows, qw), lambda b, i: (cur(b, i), 0)),
        out_shape=jax.ShapeDtypeStruct((t, qw), BF16),
        compiler_params=_params(("arbitrary", "arbitrary")),
        name="swa",
    )(sinks, z, z, z, z, z, z)


def kernel(x, norm_gains, w_in, b_gates, w_decay_up, b_decay, gla_norm_gains, sinks,
           w_gla_out, w_swa_out, w_out, final_norm_gain):
    batch, seq, d = x.shape
    depth = w_in.shape[0]
    xt = x.reshape(batch * seq, d)
    w_in_t = jnp.swapaxes(w_in, 1, 2)
    to_cast = (w_gla_out.reshape(depth * GLA_V_WIDTH, d), w_swa_out.reshape(depth * SWA_Q_WIDTH, d),
               w_out.reshape(depth * d, d))
    pad_dec = DECAY_PAD - GLA_DECAY_RANK
    w_decs = [jnp.pad(w_in_t[l, W_IN_DECAY:W_IN_SQ, :], ((0, pad_dec), (0, 0))) for l in range(depth)]
    w_ups = [jnp.pad(w_decay_up[l], ((0, pad_dec), (0, 0))).astype(BF16) for l in range(depth)]
    for l in range(depth):
        if l == 0:
            h, la = norm_decay(xt, norm_gains[l][None, :], w_decs[l], w_ups[l], b_decay[l][None, :])
            z, w_ga, w_sw, w_o, w_in_rest = in_proj(
                h, w_in_t, l, tm=2048, tn=512, cast=to_cast,
                pack_layers=(1, depth - 1) if depth > 1 else None,
                pack_gains=norm_gains[:, None, :]) + ((None,) if depth == 1 else ())
        else:
            rinv, la = decay(gdec, ssq, w_ups[l], b_decay[l][None, :], d=d)
            z = in_proj(xb, w_in_rest, l - 1, tm=2048, tn=1024, packed=True, rinv=rinv)
        ua = gla(z, la, gla_norm_gains[l][None, :], batch=batch)
        ub = swa(z, sinks[l], batch=batch)
        merged = merge(ua, ub, w_ga, w_sw, z, b_gates[l], l)
        if l + 1 < depth:
            xt, xb, ssq, gdec = out_proj_stats(merged, w_o, xt, l, w_decs[l + 1],
                                               norm_gains[l + 1][None, :])
        else:
            xt = out_proj(merged, w_o, xt, l)
    out = final_norm(xt, final_norm_gain[None, :])
    return out.reshape(batch, seq, d)
```

```python
import functools

import jax
import jax.numpy as jnp
from jax import lax
from jax.experimental import pallas as pl
from jax.experimental.pallas import tpu as pltpu

F32 = jnp.float32
BF16 = jnp.bfloat16

D_MODEL = 4096
GLA_HEADS = 8
GLA_DK = 128
GLA_DV = 256
GLA_QK_WIDTH = GLA_HEADS * GLA_DK
GLA_V_WIDTH = GLA_HEADS * GLA_DV
GLA_DECAY_RANK = 16
GLA_GATE_NORMALIZER = 16.0
GLA_LOG_DECAY_MIN = -1.0
GLA_CHUNK = 64
GLA_AHEAD = 2
SWA_Q_HEADS = 32
SWA_KV_HEADS = 8
SWA_HEAD_DIM = 64
SWA_WINDOW = 128
SWA_AHEAD = 2
SWA_Q_WIDTH = SWA_Q_HEADS * SWA_HEAD_DIM
SWA_KV_WIDTH = SWA_KV_HEADS * SWA_HEAD_DIM
RMS_EPS = 1e-6

LANES = 128
DECAY_PAD = LANES

W_IN_DECAY = 2 * GLA_QK_WIDTH + 2 * GLA_V_WIDTH
W_IN_SQ = W_IN_DECAY + GLA_DECAY_RANK
W_IN_SK = W_IN_SQ + SWA_Q_WIDTH
W_IN_SG = W_IN_SK + 2 * SWA_KV_WIDTH
W_IN_GA = W_IN_SG + SWA_Q_WIDTH
Z_GQ = 0
Z_GK = Z_GQ + GLA_QK_WIDTH
Z_GV = Z_GK + GLA_QK_WIDTH
Z_GG = Z_GV + GLA_V_WIDTH
Z_SQ = Z_GG + GLA_V_WIDTH
Z_SG = Z_SQ + SWA_Q_WIDTH
Z_SK = Z_SG + SWA_Q_WIDTH
Z_SV = Z_SK + SWA_KV_WIDTH
Z_GA = Z_SV + SWA_KV_WIDTH
Z_GB = Z_GA + D_MODEL
Z_WIDTH = Z_GB + D_MODEL

VMEM_LIMIT_BYTES = 56 * 1024 * 1024
IN_PROJ_VMEM_LIMIT_BYTES = 61 * 1024 * 1024

NEG_BIG = -1e30
LOG2E = 1.4426950408889634
NT_DIMS = (((1,), (1,)), ((), ()))


def _params(semantics, vmem_limit_bytes=VMEM_LIMIT_BYTES):
    return pltpu.CompilerParams(dimension_semantics=semantics,
                                vmem_limit_bytes=vmem_limit_bytes)


def _rms_rows(x, gain):
    ms = jnp.mean(x * x, axis=-1, keepdims=True)
    return x * lax.rsqrt(ms + RMS_EPS) * gain


def _log_sigmoid(x):
    return jnp.minimum(x, 0.0) - jnp.log(1.0 + jnp.exp(-jnp.abs(x)))


NORM_SUB_ROWS = 128


def _norm_decay_kernel(x_ref, g_ref, wdec_ref, wup_ref, bdec_ref, h_ref, la_ref):
    c_sz = GLA_CHUNK
    ri = lax.broadcasted_iota(jnp.int32, (c_sz, c_sz), 0)
    ci = lax.broadcasted_iota(jnp.int32, (c_sz, c_sz), 1)
    tril = (ci <= ri).astype(BF16)
    gain = g_ref[...]
    wdec = wdec_ref[...].astype(BF16)
    for t in range(x_ref.shape[0] // NORM_SUB_ROWS):
        rows = slice(t * NORM_SUB_ROWS, (t + 1) * NORM_SUB_ROWS)
        h = _rms_rows(x_ref[rows, :], gain).astype(BF16)
        h_ref[rows, :] = h
        gdec = lax.dot_general(h, wdec, NT_DIMS, preferred_element_type=F32)
        logit = jnp.dot(gdec.astype(BF16), wup_ref[...], preferred_element_type=F32) + bdec_ref[...]
        la = jnp.maximum(_log_sigmoid(logit) / GLA_GATE_NORMALIZER, GLA_LOG_DECAY_MIN)
        hi = la.astype(BF16)
        r1 = la - hi.astype(F32)
        mid = r1.astype(BF16)
        lo = (r1 - mid.astype(F32)).astype(BF16)
        for c in range(NORM_SUB_ROWS // c_sz):
            r = slice(c * c_sz, (c + 1) * c_sz)
            b = (jnp.dot(tril, hi[r], preferred_element_type=F32)
                 + jnp.dot(tril, mid[r], preferred_element_type=F32)
                 + jnp.dot(tril, lo[r], preferred_element_type=F32))
            la_ref[t * NORM_SUB_ROWS + c * c_sz:t * NORM_SUB_ROWS + (c + 1) * c_sz, :] = b * LOG2E


def norm_decay(x, gain, wdec, wup, bdec, *, tm=512):
    t, d = x.shape
    tm = min(tm, t)
    return pl.pallas_call(
        _norm_decay_kernel,
        grid=(t // tm,),
        in_specs=[
            pl.BlockSpec((tm, d), lambda i: (i, 0)),
            pl.BlockSpec((1, d), lambda i: (0, 0)),
            pl.BlockSpec((DECAY_PAD, d), lambda i: (0, 0)),
            pl.BlockSpec((DECAY_PAD, GLA_QK_WIDTH), lambda i: (0, 0)),
            pl.BlockSpec((1, GLA_QK_WIDTH), lambda i: (0, 0)),
        ],
        out_specs=[
            pl.BlockSpec((tm, d), lambda i: (i, 0)),
            pl.BlockSpec((tm, GLA_QK_WIDTH), lambda i: (i, 0)),
        ],
        out_shape=[
            jax.ShapeDtypeStruct((t, d), BF16),
            jax.ShapeDtypeStruct((t, GLA_QK_WIDTH), F32),
        ],
        compiler_params=_params(("arbitrary",)),
        name="norm_decay",
    )(x, gain, wdec, wup, bdec)


def _final_norm_kernel(x_ref, g_ref, o_ref):
    o_ref[...] = _rms_rows(x_ref[...], g_ref[...])


def final_norm(x, gain, *, tm=512):
    t, d = x.shape
    tm = min(tm, t)
    return pl.pallas_call(
        _final_norm_kernel,
        grid=(t // tm,),
        in_specs=[pl.BlockSpec((tm, d), lambda i: (i, 0)),
                  pl.BlockSpec((1, d), lambda i: (0, 0))],
        out_specs=pl.BlockSpec((tm, d), lambda i: (i, 0)),
        out_shape=jax.ShapeDtypeStruct((t, d), F32),
        compiler_params=_params(("arbitrary",)),
        name="final_norm",
    )(x, gain)


def _in_proj_kernel(h_ref, wt_ref, z_ref):
    w = wt_ref[0].astype(BF16)
    z_ref[...] = lax.dot_general(h_ref[...], w, NT_DIMS,
                                 preferred_element_type=F32).astype(z_ref.dtype)


def _w_in_row(j, tn):
    u = GLA_DECAY_RANK
    sg_lo, sk_lo, ga_lo = Z_SG // tn, Z_SK // tn, Z_GA // tn
    col = j * (tn // u)
    units = jnp.where(j < Z_SQ // tn, col,
                      jnp.where(j < sg_lo, col + (W_IN_SQ - Z_SQ) // u,
                                jnp.where(j < sk_lo, col + (W_IN_SG - Z_SG) // u,
                                          jnp.where(j < ga_lo, col + (W_IN_SK - Z_SK) // u,
                                                    col + (W_IN_GA - Z_GA) // u))))
    return units * u


def _in_proj_cast_kernel(h_ref, wt_ref, *refs):
    n = (len(refs) - 1) // 2
    _in_proj_kernel(h_ref, wt_ref, refs[n])
    for src, dst in zip(refs[:n], refs[n + 1:]):
        dst[...] = src[...].astype(dst.dtype)


def _cast_rows_per_step(rows, n_steps):
    sub = 16
    r = sub
    while rows % r or rows // r > n_steps:
        r += sub
    return r


PACK_ROWS = (256, 512, 1024)


def in_proj(h, wt, layer, *, tm, tn, packed=False, cast=(), pack_layers=None):
    t, d = h.shape
    tm = min(tm, t)
    n_j = Z_WIDTH // tn
    n_steps = (t // tm) * n_j
    step = lambda i, j: i * n_j + j
    if packed:
        w_spec = pl.BlockSpec((1, tn, d), lambda i, j: (layer, j, 0))
    else:
        w_spec = pl.BlockSpec((pl.Element(1), pl.Element(tn), pl.Element(d)),
                              lambda i, j: (layer, _w_in_row(j, tn), 0))
    in_specs = [
        pl.BlockSpec((tm, d), lambda i, j: (i, 0), pipeline_mode=pl.Buffered(1)),
        w_spec]
    out_specs = [pl.BlockSpec((tm, tn), lambda i, j: (i, j))]
    out_shape = [jax.ShapeDtypeStruct((t, Z_WIDTH), BF16)]
    operands = [h, wt]
    cast_in, cast_out = [], []
    for w in cast:
        rows, cols = w.shape
        r = _cast_rows_per_step(rows, n_steps)
        last = rows // r - 1
        spec = pl.BlockSpec((r, cols), lambda i, j, last=last: (jnp.minimum(step(i, j), last), 0))
        cast_in.append(spec)
        cast_out.append(spec)
        out_shape.append(jax.ShapeDtypeStruct((rows, cols), BF16))
        operands.append(w)
    if pack_layers is not None:
        first, count = pack_layers
        pr = next(r for r in PACK_ROWS if count * (Z_WIDTH // r) <= n_steps)
        nb = Z_WIDTH // pr
        last = count * nb - 1

        def blk(i, j):
            s_ = jnp.minimum(step(i, j), last)
            return s_ // nb, s_ % nb

        cast_in.append(pl.BlockSpec(
            (pl.Element(1), pl.Element(pr), pl.Element(d)),
            lambda i, j: (first + blk(i, j)[0], _w_in_row(blk(i, j)[1], pr), 0)))
        cast_out.append(pl.BlockSpec((1, pr, d), lambda i, j: (blk(i, j)[0], blk(i, j)[1], 0)))
        out_shape.append(jax.ShapeDtypeStruct((count, Z_WIDTH, d), BF16))
        operands.append(wt)
    has_cast = bool(cast_in)
    return pl.pallas_call(
        _in_proj_cast_kernel if has_cast else _in_proj_kernel,
        grid=(t // tm, n_j),
        in_specs=in_specs + cast_in,
        out_specs=out_specs + cast_out if has_cast else out_specs[0],
        out_shape=out_shape if has_cast else out_shape[0],
        compiler_params=_params(("arbitrary", "arbitrary"), IN_PROJ_VMEM_LIMIT_BYTES),
        name="in_proj",
    )(*operands)


def _out_proj_kernel(m_ref, w_ref, x_ref, o_ref):
    o_ref[...] = x_ref[...] + jnp.dot(m_ref[...], w_ref[...], preferred_element_type=F32)


def out_proj(merged, w, x, layer, *, tm=1024, tn=1024):
    t, d = merged.shape
    tm = min(tm, t)
    n = w.shape[1]
    return pl.pallas_call(
        _out_proj_kernel,
        grid=(t // tm, n // tn),
        in_specs=[pl.BlockSpec((tm, d), lambda i, j: (i, 0)),
                  pl.BlockSpec((d, tn), lambda i, j: (layer, j)),
                  pl.BlockSpec((tm, tn), lambda i, j: (i, j))],
        out_specs=pl.BlockSpec((tm, tn), lambda i, j: (i, j)),
        out_shape=jax.ShapeDtypeStruct((t, n), F32),
        compiler_params=_params(("arbitrary", "arbitrary")),
        name="out_proj",
    )(merged, w, x)


MERGE_ROW_SPLIT = 4


def _merge_kernel(ua_ref, ub_ref, wa_ref, wb_ref, ga_ref, gb_ref, bg_ref, o_ref, t_ref):
    slot = jnp.minimum(pl.program_id(0), 0)
    bg = bg_ref[...]
    sub = o_ref.shape[0] // MERGE_ROW_SPLIT
    for t in range(MERGE_ROW_SPLIT):
        rows = slice(t * sub, (t + 1) * sub)
        ya = jnp.dot(ua_ref[rows, :], wa_ref[...], preferred_element_type=F32)
        t_ref[slot] = jax.nn.sigmoid(ga_ref[rows, :].astype(F32) + bg[0:1, :]) * ya
        yb = jnp.dot(ub_ref[rows, :], wb_ref[...], preferred_element_type=F32)
        sb = jax.nn.sigmoid(gb_ref[rows, :].astype(F32) + bg[1:2, :])
        o_ref[rows, :] = (t_ref[slot] + sb * yb).astype(o_ref.dtype)


def merge(ua, ub, wa, wb, z, bgates, layer, *, tm=1024, tn=1024):
    t, ka = ua.shape
    tm = min(tm, t)
    kb = ub.shape[1]
    n = wa.shape[1]
    ga_blk = Z_GA // tn
    gb_blk = Z_GB // tn
    return pl.pallas_call(
        _merge_kernel,
        grid=(t // tm, n // tn),
        in_specs=[pl.BlockSpec((tm, ka), lambda i, j: (i, 0)),
                  pl.BlockSpec((tm, kb), lambda i, j: (i, 0)),
                  pl.BlockSpec((ka, tn), lambda i, j: (layer, j)),
                  pl.BlockSpec((kb, tn), lambda i, j: (layer, j)),
                  pl.BlockSpec((tm, tn), lambda i, j: (i, ga_blk + j)),
                  pl.BlockSpec((tm, tn), lambda i, j: (i, gb_blk + j)),
                  pl.BlockSpec((2, tn), lambda i, j: (0, j))],
        out_specs=pl.BlockSpec((tm, tn), lambda i, j: (i, j)),
        out_shape=jax.ShapeDtypeStruct((t, n), BF16),
        scratch_shapes=[pltpu.VMEM((1, tm // MERGE_ROW_SPLIT, tn), F32)],
        compiler_params=_params(("arbitrary", "arbitrary")),
        name="merge",
    )(ua, ub, wa, wb, z, z, bgates)


def _gla_kernel(q_ref, k_ref, v_ref, gg_ref, b_ref_in, gain_ref, o_ref, st_ref, *, rows):
    @pl.when(pl.program_id(1) == 0)
    def _():
        st_ref[...] = jnp.zeros_like(st_ref)

    c_sz = GLA_CHUNK
    scale = GLA_DK ** -0.5
    ri = lax.broadcasted_iota(jnp.int32, (c_sz, c_sz), 0)
    ci = lax.broadcasted_iota(jnp.int32, (c_sz, c_sz), 1)
    causal = ci <= ri
    gain = gain_ref[...]
    tn = (((0,), (0,)), ((), ()))

    def stage_a(c, h):
        kcols = slice(h * GLA_DK, (h + 1) * GLA_DK)
        vcols = slice(h * GLA_DV, (h + 1) * GLA_DV)
        r = slice(c * c_sz, (c + 1) * c_sz)
        qf = q_ref[r, kcols].astype(F32) * scale
        kf = k_ref[r, kcols].astype(F32)
        v = v_ref[r, vcols]
        b = b_ref_in[r, kcols]
        b_mid = b[c_sz // 2:c_sz // 2 + 1, :]
        b_last = b[c_sz - 1:c_sz, :]
        decay = jnp.exp2(b_last)
        q_dec = qf * jnp.exp2(b)
        k_grow = kf * jnp.exp2(-b)
        q_intra = (q_dec * jnp.exp2(-b_mid)).astype(BF16)
        k_intra = (k_grow * jnp.exp2(b_mid)).astype(BF16)
        q_inter = q_dec.astype(BF16)
        k_state = (k_grow * decay).astype(BF16)

        a = lax.dot_general(q_intra, k_intra, NT_DIMS, preferred_element_type=F32)
        a = jnp.where(causal, a, 0.0).astype(BF16)
        st = st_ref[h]
        o_inter = lax.dot_general(q_inter, st.astype(BF16), NT_DIMS, preferred_element_type=F32)
        st_ref[h] = st * decay + lax.dot_general(v, k_state, tn, preferred_element_type=F32)
        return a, v, o_inter

    def stage_b(c, h, a, v, o_inter):
        vcols = slice(h * GLA_DV, (h + 1) * GLA_DV)
        r = slice(c * c_sz, (c + 1) * c_sz)
        o = jnp.dot(a, v, preferred_element_type=F32) + o_inter
        y = _rms_rows(o, gain)
        g = gg_ref[r, vcols].astype(F32)
        o_ref[r, vcols] = (y * (g * jax.nn.sigmoid(g))).astype(o_ref.dtype)

    work = [(c, h) for c in range(rows // c_sz) for h in range(GLA_HEADS)]
    ahead = GLA_AHEAD
    pending = [stage_a(*work[i]) for i in range(ahead)]
    for i, (c, h) in enumerate(work):
        if i + ahead < len(work):
            pending.append(stage_a(*work[i + ahead]))
        stage_b(c, h, *pending.pop(0))


def gla(z, la, gain, *, batch, rows=1024):
    t = z.shape[0]
    rows = min(rows, t // batch)
    steps = t // batch // rows
    qw, vw = GLA_QK_WIDTH, GLA_V_WIDTH
    row_map = lambda b, i: b * steps + i
    return pl.pallas_call(
        functools.partial(_gla_kernel, rows=rows),
        grid=(batch, steps),
        in_specs=[
            pl.BlockSpec((rows, qw), lambda b, i: (row_map(b, i), Z_GQ // qw)),
            pl.BlockSpec((rows, qw), lambda b, i: (row_map(b, i), Z_GK // qw)),
            pl.BlockSpec((rows, vw), lambda b, i: (row_map(b, i), Z_GV // vw)),
            pl.BlockSpec((rows, vw), lambda b, i: (row_map(b, i), Z_GG // vw)),
            pl.BlockSpec((rows, qw), lambda b, i: (row_map(b, i), 0)),
            pl.BlockSpec((1, GLA_DV), lambda b, i: (0, 0)),
        ],
        out_specs=pl.BlockSpec((rows, vw), lambda b, i: (row_map(b, i), 0)),
        out_shape=jax.ShapeDtypeStruct((t, vw), BF16),
        scratch_shapes=[pltpu.VMEM((GLA_HEADS, GLA_DV, GLA_DK), F32)],
        compiler_params=_params(("arbitrary", "arbitrary")),
        name="gla",
    )(z, z, z, z, la, gain)


def _swa_kernel(sink_ref, q_ref, kp_ref, kc_ref, vp_ref, vc_ref, sg_ref, o_ref, *, blocks):
    w = SWA_WINDOW
    hd = SWA_HEAD_DIM
    rep = SWA_Q_HEADS // SWA_KV_HEADS
    scale = hd ** -0.5 * LOG2E
    first_prev_key = jnp.where(pl.program_id(1) > 0, 0, w)

    kj = lax.broadcasted_iota(jnp.int32, (w, w), 0)
    qi = lax.broadcasted_iota(jnp.int32, (w, w), 1)
    upper = kj > qi
    use_prev_first = upper & (kj >= first_prev_key)
    lane = lax.broadcasted_iota(jnp.int32, (w, LANES), 1)
    head_lanes = (lane < hd, lane >= hd)
    lane2 = lax.broadcasted_iota(jnp.int32, (2 * w, LANES), 1)
    half_lanes = (lane2 < hd, lane2 >= hd)
    tn_dims = (((0,), (0,)), ((), ()))

    def band(prev_ref, cur_ref, blk, cols):
        cur = cur_ref[blk * w:(blk + 1) * w, cols]
        prev = prev_ref[:, cols] if blk == 0 else cur_ref[(blk - 1) * w:blk * w, cols]
        return jnp.concatenate([prev, cur], axis=0)

    def scores(blk, g):
        pair, sub = divmod(g, 2)
        cols = slice(pair * LANES, (pair + 1) * LANES)
        k2 = band(kp_ref, kc_ref, blk, cols) * scale
        kg = jnp.where(half_lanes[sub], k2, pltpu.roll(k2, hd, axis=1))
        qs = []
        for hp in range(rep // 2):
            head0 = g * rep + 2 * hp
            q2 = q_ref[blk * w:(blk + 1) * w, head0 * hd:(head0 + 2) * hd]
            for e in range(2):
                qs.append(jnp.where(head_lanes[e], q2, jnp.zeros_like(q2)))
        return lax.dot_general(kg, jnp.concatenate(qs, axis=0), NT_DIMS,
                               preferred_element_type=F32)

    work = [(blk, g) for blk in range(blocks) for g in range(SWA_KV_HEADS)]
    pending = [scores(*work[i]) for i in range(SWA_AHEAD)]
    for n, (blk, g) in enumerate(work):
        st = pending.pop(0)
        if n + SWA_AHEAD < len(work):
            pending.append(scores(*work[n + SWA_AHEAD]))
        pair, sub = divmod(g, 2)
        cols = slice(pair * LANES, (pair + 1) * LANES)
        qrows = slice(blk * w, (blk + 1) * w)
        v2 = band(vp_ref, vc_ref, blk, cols)
        use_prev = use_prev_first if blk == 0 else upper
        ps, invs = [], []
        for i in range(rep):
            sblk = st[:, i * w:(i + 1) * w]
            s = jnp.where(use_prev, sblk[:w], jnp.where(upper, NEG_BIG, sblk[w:]))
            sink = sink_ref[g * rep + i] * LOG2E
            m = jnp.maximum(jnp.max(s, axis=0, keepdims=True), sink)
            p = jnp.exp2(s - m)
            denom = jnp.sum(p, axis=0, keepdims=True) + jnp.exp2(sink - m)
            invs.append(1.0 / denom)
            ps.append(jnp.concatenate([jnp.where(upper, p, 0.0), jnp.where(upper, 0.0, p)],
                                      axis=0).astype(BF16))
        ot = lax.dot_general(v2, jnp.concatenate(ps, axis=1), tn_dims,
                             preferred_element_type=F32)
        drows = slice(sub * hd, (sub + 1) * hd)
        for hp in range(rep // 2):
            head0 = g * rep + 2 * hp
            qcols = slice(head0 * hd, (head0 + 2) * hd)
            o_a = ot[drows, (2 * hp) * w:(2 * hp + 1) * w] * invs[2 * hp]
            o_b = ot[drows, (2 * hp + 1) * w:(2 * hp + 2) * w] * invs[2 * hp + 1]
            o = jnp.concatenate([o_a, o_b], axis=0).T
            gate = sg_ref[qrows, qcols].astype(F32)
            o_ref[qrows, qcols] = (o * (gate * jax.nn.sigmoid(gate))).astype(o_ref.dtype)


def swa(z, sinks, *, batch, blocks=8):
    t = z.shape[0]
    w = SWA_WINDOW
    blocks = min(blocks, t // batch // w)
    rows = blocks * w
    steps = t // batch // rows
    qw, kw = SWA_Q_WIDTH, SWA_KV_WIDTH
    cur = lambda b, i: b * steps + i
    prev = lambda b, i: (b * steps + i) * blocks - jnp.where(i > 0, 1, 0)
    return pl.pallas_call(
        functools.partial(_swa_kernel, blocks=blocks),
        grid=(batch, steps),
        in_specs=[
            pl.BlockSpec(memory_space=pltpu.SMEM),
            pl.BlockSpec((rows, qw), lambda b, i: (cur(b, i), Z_SQ // qw)),
            pl.BlockSpec((w, kw), lambda b, i: (prev(b, i), Z_SK // kw)),
            pl.BlockSpec((rows, kw), lambda b, i: (cur(b, i), Z_SK // kw)),
            pl.BlockSpec((w, kw), lambda b, i: (prev(b, i), Z_SV // kw)),
            pl.BlockSpec((rows, kw), lambda b, i: (cur(b, i), Z_SV // kw)),
            pl.BlockSpec((rows, qw), lambda b, i: (cur(b, i), Z_SG // qw)),
        ],
        out_specs=pl.BlockSpec((rows, qw), lambda b, i: (cur(b, i), 0)),
        out_shape=jax.ShapeDtypeStruct((t, qw), BF16),
        compiler_params=_params(("arbitrary", "arbitrary")),
        name="swa",
    )(sinks, z, z, z, z, z, z)


def kernel(x, norm_gains, w_in, b_gates, w_decay_up, b_decay, gla_norm_gains, sinks,
           w_gla_out, w_swa_out, w_out, final_norm_gain):
    batch, seq, d = x.shape
    depth = w_in.shape[0]
    xt = x.reshape(batch * seq, d)
    w_in_t = jnp.swapaxes(w_in, 1, 2)
    to_cast = (w_gla_out.reshape(depth * GLA_V_WIDTH, d), w_swa_out.reshape(depth * SWA_Q_WIDTH, d),
               w_out.reshape(depth * d, d))
    pad_dec = DECAY_PAD - GLA_DECAY_RANK
    for l in range(depth):
        w_dec = jnp.pad(w_in_t[l, W_IN_DECAY:W_IN_SQ, :], ((0, pad_dec), (0, 0)))
        w_up = jnp.pad(w_decay_up[l], ((0, pad_dec), (0, 0))).astype(BF16)
        h, la = norm_decay(xt, norm_gains[l][None, :], w_dec, w_up, b_decay[l][None, :])
        if l == 0:
            z, w_ga, w_sw, w_o, w_in_rest = in_proj(
                h, w_in_t, l, tm=2048, tn=512, cast=to_cast,
                pack_layers=(1, depth - 1) if depth > 1 else None) + ((None,) if depth == 1 else ())
        else:
            z = in_proj(h, w_in_rest, l - 1, tm=2048, tn=1024, packed=True)
        ua = gla(z, la, gla_norm_gains[l][None, :], batch=batch)
        ub = swa(z, sinks[l], batch=batch)
        merged = merge(ua, ub, w_ga, w_sw, z, b_gates[l], l)
        xt = out_proj(merged, w_o, xt, l)
    out = final_norm(xt, final_norm_gain[None, :])
    return out.reshape(batch, seq, d)
```

```python
import functools

import jax
import jax.numpy as jnp
from jax import lax
from jax.experimental import pallas as pl
from jax.experimental.pallas import tpu as pltpu

F32 = jnp.float32
BF16 = jnp.bfloat16

D_MODEL = 4096
GLA_HEADS = 8
GLA_DK = 128
GLA_DV = 256
GLA_QK_WIDTH = GLA_HEADS * GLA_DK
GLA_V_WIDTH = GLA_HEADS * GLA_DV
GLA_DECAY_RANK = 16
GLA_GATE_NORMALIZER = 16.0
GLA_LOG_DECAY_MIN = -1.0
GLA_CHUNK = 64
GLA_AHEAD = 2
SWA_Q_HEADS = 32
SWA_KV_HEADS = 8
SWA_HEAD_DIM = 64
SWA_WINDOW = 128
SWA_AHEAD = 2
SWA_Q_WIDTH = SWA_Q_HEADS * SWA_HEAD_DIM
SWA_KV_WIDTH = SWA_KV_HEADS * SWA_HEAD_DIM
RMS_EPS = 1e-6

LANES = 128
DECAY_PAD = LANES

W_IN_DECAY = 2 * GLA_QK_WIDTH + 2 * GLA_V_WIDTH
W_IN_SQ = W_IN_DECAY + GLA_DECAY_RANK
W_IN_SK = W_IN_SQ + SWA_Q_WIDTH
W_IN_SG = W_IN_SK + 2 * SWA_KV_WIDTH
W_IN_GA = W_IN_SG + SWA_Q_WIDTH
Z_GQ = 0
Z_GK = Z_GQ + GLA_QK_WIDTH
Z_GV = Z_GK + GLA_QK_WIDTH
Z_GG = Z_GV + GLA_V_WIDTH
Z_SQ = Z_GG + GLA_V_WIDTH
Z_SG = Z_SQ + SWA_Q_WIDTH
Z_SK = Z_SG + SWA_Q_WIDTH
Z_SV = Z_SK + SWA_KV_WIDTH
Z_GA = Z_SV + SWA_KV_WIDTH
Z_GB = Z_GA + D_MODEL
Z_WIDTH = Z_GB + D_MODEL

VMEM_LIMIT_BYTES = 56 * 1024 * 1024
IN_PROJ_VMEM_LIMIT_BYTES = 61 * 1024 * 1024

NEG_BIG = -1e30
LOG2E = 1.4426950408889634
NT_DIMS = (((1,), (1,)), ((), ()))


def _params(semantics, vmem_limit_bytes=VMEM_LIMIT_BYTES):
    return pltpu.CompilerParams(dimension_semantics=semantics,
                                vmem_limit_bytes=vmem_limit_bytes)


def _rms_rows(x, gain):
    ms = jnp.mean(x * x, axis=-1, keepdims=True)
    return x * lax.rsqrt(ms + RMS_EPS) * gain


def _log_sigmoid(x):
    return jnp.minimum(x, 0.0) - jnp.log(1.0 + jnp.exp(-jnp.abs(x)))


NORM_SUB_ROWS = 128


def _norm_decay_kernel(x_ref, g_ref, wdec_ref, wup_ref, bdec_ref, h_ref, la_ref):
    c_sz = GLA_CHUNK
    ri = lax.broadcasted_iota(jnp.int32, (c_sz, c_sz), 0)
    ci = lax.broadcasted_iota(jnp.int32, (c_sz, c_sz), 1)
    tril = (ci <= ri).astype(BF16)
    gain = g_ref[...]
    wdec = wdec_ref[...].astype(BF16)
    for t in range(x_ref.shape[0] // NORM_SUB_ROWS):
        rows = slice(t * NORM_SUB_ROWS, (t + 1) * NORM_SUB_ROWS)
        h = _rms_rows(x_ref[rows, :], gain).astype(BF16)
        h_ref[rows, :] = h
        gdec = lax.dot_general(h, wdec, NT_DIMS, preferred_element_type=F32)
        logit = jnp.dot(gdec.astype(BF16), wup_ref[...], preferred_element_type=F32) + bdec_ref[...]
        la = jnp.maximum(_log_sigmoid(logit) / GLA_GATE_NORMALIZER, GLA_LOG_DECAY_MIN)
        hi = la.astype(BF16)
        r1 = la - hi.astype(F32)
        mid = r1.astype(BF16)
        lo = (r1 - mid.astype(F32)).astype(BF16)
        for c in range(NORM_SUB_ROWS // c_sz):
            r = slice(c * c_sz, (c + 1) * c_sz)
            b = (jnp.dot(tril, hi[r], preferred_element_type=F32)
                 + jnp.dot(tril, mid[r], preferred_element_type=F32)
                 + jnp.dot(tril, lo[r], preferred_element_type=F32))
            la_ref[t * NORM_SUB_ROWS + c * c_sz:t * NORM_SUB_ROWS + (c + 1) * c_sz, :] = b * LOG2E


def norm_decay(x, gain, wdec, wup, bdec, *, tm=512):
    t, d = x.shape
    tm = min(tm, t)
    return pl.pallas_call(
        _norm_decay_kernel,
        grid=(t // tm,),
        in_specs=[
            pl.BlockSpec((tm, d), lambda i: (i, 0)),
            pl.BlockSpec((1, d), lambda i: (0, 0)),
            pl.BlockSpec((DECAY_PAD, d), lambda i: (0, 0)),
            pl.BlockSpec((DECAY_PAD, GLA_QK_WIDTH), lambda i: (0, 0)),
            pl.BlockSpec((1, GLA_QK_WIDTH), lambda i: (0, 0)),
        ],
        out_specs=[
            pl.BlockSpec((tm, d), lambda i: (i, 0)),
            pl.BlockSpec((tm, GLA_QK_WIDTH), lambda i: (i, 0)),
        ],
        out_shape=[
            jax.ShapeDtypeStruct((t, d), BF16),
            jax.ShapeDtypeStruct((t, GLA_QK_WIDTH), F32),
        ],
        compiler_params=_params(("arbitrary",)),
        name="norm_decay",
    )(x, gain, wdec, wup, bdec)


def _final_norm_kernel(x_ref, g_ref, o_ref):
    o_ref[...] = _rms_rows(x_ref[...], g_ref[...])


def final_norm(x, gain, *, tm=512):
    t, d = x.shape
    tm = min(tm, t)
    return pl.pallas_call(
        _final_norm_kernel,
        grid=(t // tm,),
        in_specs=[pl.BlockSpec((tm, d), lambda i: (i, 0)),
                  pl.BlockSpec((1, d), lambda i: (0, 0))],
        out_specs=pl.BlockSpec((tm, d), lambda i: (i, 0)),
        out_shape=jax.ShapeDtypeStruct((t, d), F32),
        compiler_params=_params(("arbitrary",)),
        name="final_norm",
    )(x, gain)


def _in_proj_kernel(h_ref, wt_ref, z_ref):
    w = wt_ref[0].astype(BF16)
    z_ref[...] = lax.dot_general(h_ref[...], w, NT_DIMS,
                                 preferred_element_type=F32).astype(z_ref.dtype)


def _w_in_row(j, tn):
    u = GLA_DECAY_RANK
    sg_lo, sk_lo, ga_lo = Z_SG // tn, Z_SK // tn, Z_GA // tn
    col = j * (tn // u)
    units = jnp.where(j < Z_SQ // tn, col,
                      jnp.where(j < sg_lo, col + (W_IN_SQ - Z_SQ) // u,
                                jnp.where(j < sk_lo, col + (W_IN_SG - Z_SG) // u,
                                          jnp.where(j < ga_lo, col + (W_IN_SK - Z_SK) // u,
                                                    col + (W_IN_GA - Z_GA) // u))))
    return units * u


def _in_proj_cast_kernel(h_ref, wt_ref, *refs):
    n = (len(refs) - 1) // 2
    _in_proj_kernel(h_ref, wt_ref, refs[n])
    for src, dst in zip(refs[:n], refs[n + 1:]):
        dst[...] = src[...].astype(dst.dtype)


def _cast_rows_per_step(rows, n_steps):
    sub = 16
    r = sub
    while rows % r or rows // r > n_steps:
        r += sub
    return r


PACK_ROWS = (256, 512, 1024)


def in_proj(h, wt, layer, *, tm, tn, packed=False, cast=(), pack_layers=None):
    t, d = h.shape
    tm = min(tm, t)
    n_j = Z_WIDTH // tn
    n_steps = (t // tm) * n_j
    step = lambda i, j: i * n_j + j
    if packed:
        w_spec = pl.BlockSpec((1, tn, d), lambda i, j: (layer, j, 0))
    else:
        w_spec = pl.BlockSpec((pl.Element(1), pl.Element(tn), pl.Element(d)),
                              lambda i, j: (layer, _w_in_row(j, tn), 0))
    in_specs = [
        pl.BlockSpec((tm, d), lambda i, j: (i, 0), pipeline_mode=pl.Buffered(1)),
        w_spec]
    out_specs = [pl.BlockSpec((tm, tn), lambda i, j: (i, j))]
    out_shape = [jax.ShapeDtypeStruct((t, Z_WIDTH), BF16)]
    operands = [h, wt]
    cast_in, cast_out = [], []
    for w in cast:
        rows, cols = w.shape
        r = _cast_rows_per_step(rows, n_steps)
        last = rows // r - 1
        spec = pl.BlockSpec((r, cols), lambda i, j, last=last: (jnp.minimum(step(i, j), last), 0))
        cast_in.append(spec)
        cast_out.append(spec)
        out_shape.append(jax.ShapeDtypeStruct((rows, cols), BF16))
        operands.append(w)
    if pack_layers is not None:
        first, count = pack_layers
        pr = next(r for r in PACK_ROWS if count * (Z_WIDTH // r) <= n_steps)
        nb = Z_WIDTH // pr
        last = count * nb - 1

        def blk(i, j):
            s_ = jnp.minimum(step(i, j), last)
            return s_ // nb, s_ % nb

        cast_in.append(pl.BlockSpec(
            (pl.Element(1), pl.Element(pr), pl.Element(d)),
            lambda i, j: (first + blk(i, j)[0], _w_in_row(blk(i, j)[1], pr), 0)))
        cast_out.append(pl.BlockSpec((1, pr, d), lambda i, j: (blk(i, j)[0], blk(i, j)[1], 0)))
        out_shape.append(jax.ShapeDtypeStruct((count, Z_WIDTH, d), BF16))
        operands.append(wt)
    has_cast = bool(cast_in)
    return pl.pallas_call(
        _in_proj_cast_kernel if has_cast else _in_proj_kernel,
        grid=(t // tm, n_j),
        in_specs=in_specs + cast_in,
        out_specs=out_specs + cast_out if has_cast else out_specs[0],
        out_shape=out_shape if has_cast else out_shape[0],
        compiler_params=_params(("arbitrary", "arbitrary"), IN_PROJ_VMEM_LIMIT_BYTES),
        name="in_proj",
    )(*operands)


def _out_proj_kernel(m_ref, w_ref, x_ref, o_ref):
    o_ref[...] = x_ref[...] + jnp.dot(m_ref[...], w_ref[...], preferred_element_type=F32)


def out_proj(merged, w, x, layer, *, tm=1024, tn=1024):
    t, d = merged.shape
    tm = min(tm, t)
    n = w.shape[1]
    return pl.pallas_call(
        _out_proj_kernel,
        grid=(t // tm, n // tn),
        in_specs=[pl.BlockSpec((tm, d), lambda i, j: (i, 0)),
                  pl.BlockSpec((d, tn), lambda i, j: (layer, j)),
                  pl.BlockSpec((tm, tn), lambda i, j: (i, j))],
        out_specs=pl.BlockSpec((tm, tn), lambda i, j: (i, j)),
        out_shape=jax.ShapeDtypeStruct((t, n), F32),
        compiler_params=_params(("arbitrary", "arbitrary")),
        name="out_proj",
    )(merged, w, x)


MERGE_ROW_SPLIT = 4


def _merge_kernel(ua_ref, ub_ref, wa_ref, wb_ref, ga_ref, gb_ref, bg_ref, o_ref, t_ref):
    slot = jnp.minimum(pl.program_id(0), 0)
    bg = bg_ref[...]
    sub = o_ref.shape[0] // MERGE_ROW_SPLIT
    for t in range(MERGE_ROW_SPLIT):
        rows = slice(t * sub, (t + 1) * sub)
        ya = jnp.dot(ua_ref[rows, :], wa_ref[...], preferred_element_type=F32)
        t_ref[slot] = jax.nn.sigmoid(ga_ref[rows, :].astype(F32) + bg[0:1, :]) * ya
        yb = jnp.dot(ub_ref[rows, :], wb_ref[...], preferred_element_type=F32)
        sb = jax.nn.sigmoid(gb_ref[rows, :].astype(F32) + bg[1:2, :])
        o_ref[rows, :] = (t_ref[slot] + sb * yb).astype(o_ref.dtype)


def merge(ua, ub, wa, wb, z, bgates, layer, *, tm=1024, tn=1024):
    t, ka = ua.shape
    tm = min(tm, t)
    kb = ub.shape[1]
    n = wa.shape[1]
    ga_blk = Z_GA // tn
    gb_blk = Z_GB // tn
    return pl.pallas_call(
        _merge_kernel,
        grid=(t // tm, n // tn),
        in_specs=[pl.BlockSpec((tm, ka), lambda i, j: (i, 0)),
                  pl.BlockSpec((tm, kb), lambda i, j: (i, 0)),
                  pl.BlockSpec((ka, tn), lambda i, j: (layer, j)),
                  pl.BlockSpec((kb, tn), lambda i, j: (layer, j)),
                  pl.BlockSpec((tm, tn), lambda i, j: (i, ga_blk + j)),
                  pl.BlockSpec((tm, tn), lambda i, j: (i, gb_blk + j)),
                  pl.BlockSpec((2, tn), lambda i, j: (0, j))],
        out_specs=pl.BlockSpec((tm, tn), lambda i, j: (i, j)),
        out_shape=jax.ShapeDtypeStruct((t, n), BF16),
        scratch_shapes=[pltpu.VMEM((1, tm // MERGE_ROW_SPLIT, tn), F32)],
        compiler_params=_params(("arbitrary", "arbitrary")),
        name="merge",
    )(ua, ub, wa, wb, z, z, bgates)


def _gla_kernel(q_ref, k_ref, v_ref, gg_ref, b_ref_in, gain_ref, o_ref, st_ref, *, rows):
    @pl.when(pl.program_id(1) == 0)
    def _():
        st_ref[...] = jnp.zeros_like(st_ref)

    c_sz = GLA_CHUNK
    scale = GLA_DK ** -0.5
    ri = lax.broadcasted_iota(jnp.int32, (c_sz, c_sz), 0)
    ci = lax.broadcasted_iota(jnp.int32, (c_sz, c_sz), 1)
    causal = ci <= ri
    gain = gain_ref[...]
    tn = (((0,), (0,)), ((), ()))

    def stage_a(c, h):
        kcols = slice(h * GLA_DK, (h + 1) * GLA_DK)
        vcols = slice(h * GLA_DV, (h + 1) * GLA_DV)
        r = slice(c * c_sz, (c + 1) * c_sz)
        qf = q_ref[r, kcols].astype(F32) * scale
        kf = k_ref[r, kcols].astype(F32)
        v = v_ref[r, vcols]
        b = b_ref_in[r, kcols]
        b_mid = b[c_sz // 2:c_sz // 2 + 1, :]
        b_last = b[c_sz - 1:c_sz, :]
        decay = jnp.exp2(b_last)
        q_dec = qf * jnp.exp2(b)
        k_grow = kf * jnp.exp2(-b)
        q_intra = (q_dec * jnp.exp2(-b_mid)).astype(BF16)
        k_intra = (k_grow * jnp.exp2(b_mid)).astype(BF16)
        q_inter = q_dec.astype(BF16)
        k_state = (k_grow * decay).astype(BF16)

        a = lax.dot_general(q_intra, k_intra, NT_DIMS, preferred_element_type=F32)
        a = jnp.where(causal, a, 0.0).astype(BF16)
        st = st_ref[h]
        o_inter = lax.dot_general(q_inter, st.astype(BF16), NT_DIMS, preferred_element_type=F32)
        st_ref[h] = st * decay + lax.dot_general(v, k_state, tn, preferred_element_type=F32)
        return a, v, o_inter

    def stage_b(c, h, a, v, o_inter):
        vcols = slice(h * GLA_DV, (h + 1) * GLA_DV)
        r = slice(c * c_sz, (c + 1) * c_sz)
        o = jnp.dot(a, v, preferred_element_type=F32) + o_inter
        y = _rms_rows(o, gain)
        g = gg_ref[r, vcols].astype(F32)
        o_ref[r, vcols] = (y * (g * jax.nn.sigmoid(g))).astype(o_ref.dtype)

    work = [(c, h) for c in range(rows // c_sz) for h in range(GLA_HEADS)]
    ahead = GLA_AHEAD
    pending = [stage_a(*work[i]) for i in range(ahead)]
    for i, (c, h) in enumerate(work):
        if i + ahead < len(work):
            pending.append(stage_a(*work[i + ahead]))
        stage_b(c, h, *pending.pop(0))


def gla(z, la, gain, *, batch, rows=1024):
    t = z.shape[0]
    rows = min(rows, t // batch)
    steps = t // batch // rows
    qw, vw = GLA_QK_WIDTH, GLA_V_WIDTH
    row_map = lambda b, i: b * steps + i
    return pl.pallas_call(
        functools.partial(_gla_kernel, rows=rows),
        grid=(batch, steps),
        in_specs=[
            pl.BlockSpec((rows, qw), lambda b, i: (row_map(b, i), Z_GQ // qw)),
            pl.BlockSpec((rows, qw), lambda b, i: (row_map(b, i), Z_GK // qw)),
            pl.BlockSpec((rows, vw), lambda b, i: (row_map(b, i), Z_GV // vw)),
            pl.BlockSpec((rows, vw), lambda b, i: (row_map(b, i), Z_GG // vw)),
            pl.BlockSpec((rows, qw), lambda b, i: (row_map(b, i), 0)),
            pl.BlockSpec((1, GLA_DV), lambda b, i: (0, 0)),
        ],
        out_specs=pl.BlockSpec((rows, vw), lambda b, i: (row_map(b, i), 0)),
        out_shape=jax.ShapeDtypeStruct((t, vw), BF16),
        scratch_shapes=[pltpu.VMEM((GLA_HEADS, GLA_DV, GLA_DK), F32)],
        compiler_params=_params(("arbitrary", "arbitrary")),
        name="gla",
    )(z, z, z, z, la, gain)


def _swa_kernel(sink_ref, q_ref, kp_ref, kc_ref, vp_ref, vc_ref, sg_ref, o_ref, st_ref, *, blocks):
    w = SWA_WINDOW
    hd = SWA_HEAD_DIM
    rep = SWA_Q_HEADS // SWA_KV_HEADS
    scale = hd ** -0.5 * LOG2E
    first_prev_key = jnp.where(pl.program_id(1) > 0, 0, w)

    kj = lax.broadcasted_iota(jnp.int32, (w, w), 0)
    qi = lax.broadcasted_iota(jnp.int32, (w, w), 1)
    upper = kj > qi
    use_prev_first = upper & (kj >= first_prev_key)
    lane = lax.broadcasted_iota(jnp.int32, (w, LANES), 1)
    head_lanes = (lane < hd, lane >= hd)
    lane2 = lax.broadcasted_iota(jnp.int32, (2 * w, LANES), 1)
    half_lanes = (lane2 < hd, lane2 >= hd)
    tn_dims = (((0,), (0,)), ((), ()))

    def band(prev_ref, cur_ref, blk, cols):
        cur = cur_ref[blk * w:(blk + 1) * w, cols]
        prev = prev_ref[:, cols] if blk == 0 else cur_ref[(blk - 1) * w:blk * w, cols]
        return jnp.concatenate([prev, cur], axis=0)

    def scores(blk, g):
        pair, sub = divmod(g, 2)
        cols = slice(pair * LANES, (pair + 1) * LANES)
        k2 = band(kp_ref, kc_ref, blk, cols) * scale
        kg = jnp.where(half_lanes[sub], k2, pltpu.roll(k2, hd, axis=1))
        qs = []
        for hp in range(rep // 2):
            head0 = g * rep + 2 * hp
            q2 = q_ref[blk * w:(blk + 1) * w, head0 * hd:(head0 + 2) * hd]
            for e in range(2):
                qs.append(jnp.where(head_lanes[e], q2, jnp.zeros_like(q2)))
        return lax.dot_general(kg, jnp.concatenate(qs, axis=0), NT_DIMS,
                               preferred_element_type=F32)

    zero = jnp.minimum(pl.program_id(0), 0)
    ring = SWA_AHEAD + 1
    work = [(blk, g) for blk in range(blocks) for g in range(SWA_KV_HEADS)]
    for i in range(SWA_AHEAD):
        st_ref[i % ring + zero] = scores(*work[i])
    for n, (blk, g) in enumerate(work):
        if n + SWA_AHEAD < len(work):
            st_ref[(n + SWA_AHEAD) % ring + zero] = scores(*work[n + SWA_AHEAD])
        st = st_ref.at[n % ring + zero]
        pair, sub = divmod(g, 2)
        cols = slice(pair * LANES, (pair + 1) * LANES)
        qrows = slice(blk * w, (blk + 1) * w)
        v2 = band(vp_ref, vc_ref, blk, cols)
        use_prev = use_prev_first if blk == 0 else upper
        ps, invs = [], []
        for i in range(rep):
            s = jnp.where(use_prev, st[0:w, i * w:(i + 1) * w],
                          jnp.where(upper, NEG_BIG, st[w:2 * w, i * w:(i + 1) * w]))
            sink = sink_ref[g * rep + i] * LOG2E
            m = jnp.maximum(jnp.max(s, axis=0, keepdims=True), sink)
            p = jnp.exp2(s - m)
            denom = jnp.sum(p, axis=0, keepdims=True) + jnp.exp2(sink - m)
            invs.append(1.0 / denom)
            ps.append(jnp.concatenate([jnp.where(upper, p, 0.0), jnp.where(upper, 0.0, p)],
                                      axis=0).astype(BF16))
        ot = lax.dot_general(v2, jnp.concatenate(ps, axis=1), tn_dims,
                             preferred_element_type=F32)
        drows = slice(sub * hd, (sub + 1) * hd)
        for hp in range(rep // 2):
            head0 = g * rep + 2 * hp
            qcols = slice(head0 * hd, (head0 + 2) * hd)
            o_a = ot[drows, (2 * hp) * w:(2 * hp + 1) * w] * invs[2 * hp]
            o_b = ot[drows, (2 * hp + 1) * w:(2 * hp + 2) * w] * invs[2 * hp + 1]
            o = jnp.concatenate([o_a, o_b], axis=0).T
            gate = sg_ref[qrows, qcols].astype(F32)
            o_ref[qrows, qcols] = (o * (gate * jax.nn.sigmoid(gate))).astype(o_ref.dtype)


def swa(z, sinks, *, batch, blocks=8):
    t = z.shape[0]
    w = SWA_WINDOW
    blocks = min(blocks, t // batch // w)
    rows = blocks * w
    steps = t // batch // rows
    qw, kw = SWA_Q_WIDTH, SWA_KV_WIDTH
    cur = lambda b, i: b * steps + i
    prev = lambda b, i: (b * steps + i) * blocks - jnp.where(i > 0, 1, 0)
    return pl.pallas_call(
        functools.partial(_swa_kernel, blocks=blocks),
        grid=(batch, steps),
        in_specs=[
            pl.BlockSpec(memory_space=pltpu.SMEM),
            pl.BlockSpec((rows, qw), lambda b, i: (cur(b, i), Z_SQ // qw)),
            pl.BlockSpec((w, kw), lambda b, i: (prev(b, i), Z_SK // kw)),
            pl.BlockSpec((rows, kw), lambda b, i: (cur(b, i), Z_SK // kw)),
            pl.BlockSpec((w, kw), lambda b, i: (prev(b, i), Z_SV // kw)),
            pl.BlockSpec((rows, kw), lambda b, i: (cur(b, i), Z_SV // kw)),
            pl.BlockSpec((rows, qw), lambda b, i: (cur(b, i), Z_SG // qw)),
        ],
        out_specs=pl.BlockSpec((rows, qw), lambda b, i: (cur(b, i), 0)),
        out_shape=jax.ShapeDtypeStruct((t, qw), BF16),
        scratch_shapes=[pltpu.VMEM((SWA_AHEAD + 1, 2 * w, (SWA_Q_HEADS // SWA_KV_HEADS) * w), F32)],
        compiler_params=_params(("arbitrary", "arbitrary")),
        name="swa",
    )(sinks, z, z, z, z, z, z)


def kernel(x, norm_gains, w_in, b_gates, w_decay_up, b_decay, gla_norm_gains, sinks,
           w_gla_out, w_swa_out, w_out, final_norm_gain):
    batch, seq, d = x.shape
    depth = w_in.shape[0]
    xt = x.reshape(batch * seq, d)
    w_in_t = jnp.swapaxes(w_in, 1, 2)
    to_cast = (w_gla_out.reshape(depth * GLA_V_WIDTH, d), w_swa_out.reshape(depth * SWA_Q_WIDTH, d),
               w_out.reshape(depth * d, d))
    pad_dec = DECAY_PAD - GLA_DECAY_RANK
    for l in range(depth):
        w_dec = jnp.pad(w_in_t[l, W_IN_DECAY:W_IN_SQ, :], ((0, pad_dec), (0, 0)))
        w_up = jnp.pad(w_decay_up[l], ((0, pad_dec), (0, 0))).astype(BF16)
        h, la = norm_decay(xt, norm_gains[l][None, :], w_dec, w_up, b_decay[l][None, :])
        if l == 0:
            z, w_ga, w_sw, w_o, w_in_rest = in_proj(
                h, w_in_t, l, tm=2048, tn=512, cast=to_cast,
                pack_layers=(1, depth - 1) if depth > 1 else None) + ((None,) if depth == 1 else ())
        else:
            z = in_proj(h, w_in_rest, l - 1, tm=2048, tn=1024, packed=True)
        ua = gla(z, la, gla_norm_gains[l][None, :], batch=batch)
        ub = swa(z, sinks[l], batch=batch)
        merged = merge(ua, ub, w_ga, w_sw, z, b_gates[l], l)
        xt = out_proj(merged, w_o, xt, l)
    out = final_norm(xt, final_norm_gain[None, :])
    return out.reshape(batch, seq, d)
```
